```python
import math
import jax
import jax.numpy as jnp
from jax import lax
import numpy as np

D_MODEL = 2048
BATCH = 16
SEQ = 2048
DEPTH = 4

D_MIX = D_MODEL
HG_DK = 128
HG_DV = 128
HG_WIDTH = D_MIX // 2
HG_HEADS = HG_WIDTH // HG_DV
HG_QK = HG_HEADS * HG_DK
HG_CHUNK = 32
DA_HEAD_DIM = 64
DA_VDIM = 2 * DA_HEAD_DIM
DA_WIDTH = D_MIX // 4
DA_HEADS = DA_WIDTH // DA_VDIM
DA_QK = DA_HEADS * 2 * DA_HEAD_DIM
SC_WIDTH = D_MIX - HG_WIDTH - DA_WIDTH
SC_GROUPS = 4
SC_GROUP_DIM = SC_WIDTH // SC_GROUPS
SC_KSIZE = 3
D_FF = ((8 * D_MODEL // 3 + 127) // 128) * 128
ROPE_THETA = 10000.0
Q_BLOCK = 128
EPS = 1e-6
IN_SPLITS = (HG_QK, HG_WIDTH, HG_QK, HG_QK, HG_WIDTH, DA_QK, DA_QK, DA_WIDTH, SC_WIDTH, SC_WIDTH, SC_WIDTH)
D_IN = sum(IN_SPLITS)

kernel_name = "hybrid_hgrn2_diffattn_shortconv_macaron"


def _rms_norm(x, gain):
    xf = x.astype(jnp.float32)
    y = xf * lax.rsqrt(jnp.mean(xf * xf, axis=-1, keepdims=True) + EPS)
    return (y * gain.astype(jnp.float32)).astype(x.dtype)


def _swiglu(h, w_gate, w_up, w_down):
    return (jax.nn.silu(h @ w_gate) * (h @ w_up)) @ w_down


def _rope(t, cos, sin):
    half = t.shape[-1] // 2
    t1, t2 = t[..., :half], t[..., half:]
    return jnp.concatenate([t1 * cos - t2 * sin, t2 * cos + t1 * sin], axis=-1)


def _split_columns(proj):
    parts = []
    start = 0
    for width in IN_SPLITS:
        parts.append(proj[..., start:start + width])
        start += width
    return parts


def _gla_chunk_scan(q, k, v, log_f):
    b_, s_, h_, dk = q.shape
    dv = v.shape[-1]
    n_chunks = s_ // HG_CHUNK

    def to_chunks(t):
        return t.astype(jnp.float32).reshape(b_, n_chunks, HG_CHUNK, h_, t.shape[-1]).transpose(1, 0, 3, 2, 4)

    lower = jnp.tril(jnp.ones((HG_CHUNK, HG_CHUNK), dtype=bool))[:, :, None]

    def step(state, inp):
        qc, kc, vc, lf = inp
        cum = jnp.cumsum(lf, axis=2)
        o_inter = jnp.einsum('bhtk,bhkv->bhtv', qc * jnp.exp(cum), state)
        rel = cum[:, :, :, None, :] - cum[:, :, None, :, :]
        decay = jnp.exp(jnp.where(lower, rel, -jnp.inf))
        scores = jnp.einsum('bhtsk,bhsk->bhts', qc[:, :, :, None, :] * decay, kc)
        o_intra = jnp.einsum('bhts,bhsv->bhtv', scores, vc)
        cum_end = cum[:, :, -1:, :]
        state = (jnp.exp(cum_end[:, :, 0, :])[..., None] * state
                 + jnp.einsum('bhsk,bhsv->bhkv', kc * jnp.exp(cum_end - cum), vc))
        return state, o_inter + o_intra

    state0 = jnp.zeros((b_, h_, dk, dv), jnp.float32)
    _, o = lax.scan(step, state0, (to_chunks(q), to_chunks(k), to_chunks(v), to_chunks(log_f)))
    return o.transpose(1, 0, 3, 2, 4).reshape(b_, s_, h_, dv)


def _hgrn2_gates(z, lb):
    zf = z.astype(jnp.float32)
    f = lb + (1.0 - lb) * jax.nn.sigmoid(zf)
    return jnp.log(f), (1.0 - lb) * jax.nn.sigmoid(-zf)


def _hgrn2_mixer(q, i, z_fwd, z_bwd, g, lb_fwd, lb_bwd, norm_gain):
    b_, s_, _ = q.shape

    def heads(t):
        return t.reshape(b_, s_, HG_HEADS, -1)

    def flip(t):
        return jnp.flip(t, axis=1)

    lf_fwd, k_fwd = _hgrn2_gates(z_fwd, lb_fwd)
    lf_bwd, k_bwd = _hgrn2_gates(z_bwd, lb_bwd)
    qh, ih = heads(q), heads(i)
    o_fwd = _gla_chunk_scan(qh, heads(k_fwd), ih, heads(lf_fwd))
    o_bwd = flip(_gla_chunk_scan(flip(qh), flip(heads(k_bwd)), flip(ih), flip(heads(lf_bwd))))
    o = _rms_norm(o_fwd + o_bwd, norm_gain.reshape(HG_HEADS, HG_DV))
    return o.reshape(b_, s_, HG_WIDTH).astype(g.dtype) * jax.nn.silu(g)


def _diff_attention(q, k, v, q_gain, k_gain, lam, cos, sin):
    b_, s_, _ = q.shape
    dt = q.dtype

    def prep(t, gain):
        t = _rms_norm(t.reshape(b_, s_, DA_HEADS, 2, DA_HEAD_DIM), gain)
        return _rope(t.astype(jnp.float32), cos, sin).astype(dt)

    qh, kh = prep(q, q_gain), prep(k, k_gain)
    vh = v.reshape(b_, s_, DA_HEADS, DA_VDIM)
    n_blocks = s_ // Q_BLOCK
    q_blocks = qh.reshape(b_, n_blocks, Q_BLOCK, DA_HEADS, 2, DA_HEAD_DIM).transpose(1, 0, 2, 3, 4, 5)
    scale = DA_HEAD_DIM ** -0.5

    def block(qb):
        s = jnp.einsum('bqhmd,bkhmd->bhmqk', qb, kh).astype(jnp.float32) * scale
        p = jax.nn.softmax(s, axis=-1)
        a = p[:, :, 0] - lam * p[:, :, 1]
        return jnp.einsum('bhqk,bkhe->bqhe', a.astype(dt), vh)

    o = lax.map(block, q_blocks)
    return o.transpose(1, 0, 2, 3, 4).reshape(b_, s_, DA_HEADS, DA_VDIM)


def _short_conv_mixer(b_gate, c_gate, u, conv_w, conv_b, norm_gain):
    b_, s_, _ = u.shape
    v = c_gate * u
    pad = SC_KSIZE // 2
    vp = jnp.pad(v, ((0, 0), (pad, pad), (0, 0)))
    y = conv_b
    for j in range(SC_KSIZE):
        y = y + vp[:, j:j + s_] * conv_w[j]
    y = b_gate * y
    y = _rms_norm(y.reshape(b_, s_, SC_GROUPS, SC_GROUP_DIM), norm_gain.reshape(SC_GROUPS, SC_GROUP_DIM))
    return y.reshape(b_, s_, SC_WIDTH)


def setup_inputs(seed: int = 0) -> dict:
    key = jax.random.key(seed)
    ks = jax.random.split(key, 32)
    f32 = jnp.float32

    def nrm(k, shape, scale):
        return jax.random.normal(k, shape, f32) * scale

    def gain(k, shape):
        return 1.0 + 0.1 * jax.random.normal(k, shape, f32)

    sd = D_MODEL ** -0.5
    sf = D_FF ** -0.5
    sm = D_MIX ** -0.5
    return {
        "x": jax.random.normal(ks[0], (BATCH, SEQ, D_MODEL), f32),
        "positions": jnp.broadcast_to(jnp.arange(SEQ, dtype=jnp.int32), (BATCH, SEQ)),
        "ffn1_norm": gain(ks[1], (DEPTH, D_MODEL)),
        "ffn1_w_gate": nrm(ks[2], (DEPTH, D_MODEL, D_FF), sd),
        "ffn1_w_up": nrm(ks[3], (DEPTH, D_MODEL, D_FF), sd),
        "ffn1_w_down": nrm(ks[4], (DEPTH, D_FF, D_MODEL), sf),
        "mix_norm": gain(ks[5], (DEPTH, D_MODEL)),
        "w_in": nrm(ks[6], (DEPTH, D_MODEL, D_IN), sd),
        "hgrn_lb_logits": nrm(ks[7], (2, DEPTH, HG_QK), 0.1),
        "hgrn_norm": gain(ks[8], (DEPTH, HG_WIDTH)),
        "da_q_norm": gain(ks[9], (DEPTH, DA_HEAD_DIM)),
        "da_k_norm": gain(ks[10], (DEPTH, DA_HEAD_DIM)),
        "da_lambda_q1": nrm(ks[11], (DEPTH, DA_HEAD_DIM), 0.1),
        "da_lambda_k1": nrm(ks[12], (DEPTH, DA_HEAD_DIM), 0.1),
        "da_lambda_q2": nrm(ks[13], (DEPTH, DA_HEAD_DIM), 0.1),
        "da_lambda_k2": nrm(ks[14], (DEPTH, DA_HEAD_DIM), 0.1),
        "da_out_norm": gain(ks[15], (DEPTH, DA_WIDTH)),
        "conv_w": nrm(ks[16], (DEPTH, SC_KSIZE, SC_WIDTH), SC_KSIZE ** -0.5),
        "conv_b": nrm(ks[17], (DEPTH, SC_WIDTH), 0.02),
        "conv_norm": gain(ks[18], (DEPTH, SC_WIDTH)),
        "w_out": nrm(ks[19], (DEPTH, D_MIX, D_MODEL), sm),
        "ffn2_norm": gain(ks[20], (DEPTH, D_MODEL)),
        "ffn2_w_gate": nrm(ks[21], (DEPTH, D_MODEL, D_FF), sd),
        "ffn2_w_up": nrm(ks[22], (DEPTH, D_MODEL, D_FF), sd),
        "ffn2_w_down": nrm(ks[23], (DEPTH, D_FF, D_MODEL), sf),
    }


def reference(x, positions, ffn1_norm, ffn1_w_gate, ffn1_w_up, ffn1_w_down, mix_norm, w_in,
              hgrn_lb_logits, hgrn_norm, da_q_norm, da_k_norm, da_lambda_q1, da_lambda_k1,
              da_lambda_q2, da_lambda_k2, da_out_norm, conv_w, conv_b, conv_norm, w_out,
              ffn2_norm, ffn2_w_gate, ffn2_w_up, ffn2_w_down):
    f32 = jnp.float32
    inv_freq = ROPE_THETA ** (-jnp.arange(0, DA_HEAD_DIM, 2, dtype=f32) / DA_HEAD_DIM)
    ang = positions.astype(f32)[..., None] * inv_freq
    cos = jnp.cos(ang)[:, :, None, None, :]
    sin = jnp.sin(ang)[:, :, None, None, :]
    lb_all = jnp.cumsum(jax.nn.softmax(hgrn_lb_logits.astype(f32), axis=1), axis=1)
    lb_all = lb_all - lb_all[:, :1]

    for layer in range(DEPTH):
        h = _rms_norm(x, ffn1_norm[layer])
        x = x + 0.5 * _swiglu(h, ffn1_w_gate[layer], ffn1_w_up[layer], ffn1_w_down[layer])

        h = _rms_norm(x, mix_norm[layer])
        (hg_q, hg_i, hg_zf, hg_zb, hg_g, da_q, da_k, da_v,
         sc_b, sc_c, sc_u) = _split_columns(h @ w_in[layer])

        y_a = _hgrn2_mixer(hg_q, hg_i, hg_zf, hg_zb, hg_g,
                           lb_all[0, layer], lb_all[1, layer], hgrn_norm[layer])

        lam_init = 0.8 - 0.6 * math.exp(-0.3 * layer)
        lam = (jnp.exp(jnp.sum(da_lambda_q1[layer].astype(f32) * da_lambda_k1[layer].astype(f32)))
               - jnp.exp(jnp.sum(da_lambda_q2[layer].astype(f32) * da_lambda_k2[layer].astype(f32)))
               + lam_init)
        o_b = _diff_attention(da_q, da_k, da_v, da_q_norm[layer], da_k_norm[layer], lam, cos, sin)
        y_b = (_rms_norm(o_b, da_out_norm[layer].reshape(DA_HEADS, DA_VDIM)) * (1.0 - lam_init)).reshape(
            x.shape[0], x.shape[1], DA_WIDTH)

        y_c = _short_conv_mixer(sc_b, sc_c, sc_u, conv_w[layer], conv_b[layer], conv_norm[layer])

        y = jnp.concatenate([y_a.astype(h.dtype), y_b.astype(h.dtype), y_c.astype(h.dtype)], axis=-1)
        x = x + y @ w_out[layer]

        h = _rms_norm(x, ffn2_norm[layer])
        x = x + 0.5 * _swiglu(h, ffn2_w_gate[layer], ffn2_w_up[layer], ffn2_w_down[layer])
    return x
```

```python
import functools
import math

import jax
import jax.numpy as jnp
import numpy as np
from jax import lax
from jax.experimental import pallas as pl
from jax.experimental.pallas import tpu as pltpu

F32 = jnp.float32
BF16 = jnp.bfloat16

EPS = 1e-6
ROPE_THETA = 10000.0
LANES = 128
HEAD_DIM = LANES
ATT_MAP_DIM = 64
HGRN_CHUNK = 128
ATT_Q_BLOCK = 256
VMEM_LIMIT_BYTES = 56 * 1024 * 1024

_NT = (((1,), (1,)), ((), ()))


def _rms_norm_rows(x, gain):
    ms = jnp.mean(x * x, axis=-1, keepdims=True)
    return x * lax.rsqrt(ms + EPS) * gain


def _tile(n, pref, quantum):
    if n <= pref:
        return n
    t = (pref // quantum) * quantum
    while n % t:
        t -= quantum
    return t


def _params(*semantics):
    return pltpu.CompilerParams(dimension_semantics=semantics, vmem_limit_bytes=VMEM_LIMIT_BYTES)


def _ffn_body(x_ref, gain_ref, wg_ref, wu_ref, wd_ref, o_ref, h_ref):
    j = pl.program_id(1)

    @pl.when(j == 0)
    def _():
        h_ref[...] = _rms_norm_rows(x_ref[...], gain_ref[...]).astype(BF16)
        o_ref[...] = jnp.zeros_like(o_ref)

    h = h_ref[...]
    g = jnp.dot(h, wg_ref[...], preferred_element_type=F32)
    u = jnp.dot(h, wu_ref[...], preferred_element_type=F32)
    a = (g * jax.nn.sigmoid(g) * u).astype(BF16)
    o_ref[...] += jnp.dot(a, wd_ref[...], preferred_element_type=F32)

    @pl.when(j == pl.num_programs(1) - 1)
    def _():
        o_ref[...] = x_ref[...] + 0.5 * o_ref[...]


def _ffn(x2, gain, wg, wu, wd):
    t, d = x2.shape
    fp = wg.shape[1]
    tm = _tile(t, 512, 8)
    tf = _tile(fp, 512, LANES)
    return pl.pallas_call(
        _ffn_body,
        out_shape=jax.ShapeDtypeStruct((t, d), F32),
        grid=(t // tm, fp // tf),
        in_specs=[
            pl.BlockSpec((tm, d), lambda i, j: (i, 0)),
            pl.BlockSpec((1, d), lambda i, j: (0, 0)),
            pl.BlockSpec((d, tf), lambda i, j: (0, j)),
            pl.BlockSpec((d, tf), lambda i, j: (0, j)),
            pl.BlockSpec((tf, d), lambda i, j: (j, 0)),
        ],
        out_specs=pl.BlockSpec((tm, d), lambda i, j: (i, 0)),
        scratch_shapes=[pltpu.VMEM((tm, d), BF16)],
        compiler_params=_params("parallel", "arbitrary"),
        name="ffn",
    )(x2, gain.reshape(1, d), wg, wu, wd)


def _proj_in_body(x_ref, gain_ref, w_ref, o_ref, h_ref):
    @pl.when(pl.program_id(1) == 0)
    def _():
        h_ref[...] = _rms_norm_rows(x_ref[...], gain_ref[...]).astype(BF16)

    o_ref[...] = jnp.dot(h_ref[...], w_ref[...], preferred_element_type=F32)


def _proj_in(x2, gain, w):
    t, d = x2.shape
    n = w.shape[1]
    tm = _tile(t, 1024, 8)
    tn = _tile(n, 1024, LANES)
    return pl.pallas_call(
        _proj_in_body,
        out_shape=jax.ShapeDtypeStruct((t, n), F32),
        grid=(t // tm, n // tn),
        in_specs=[
            pl.BlockSpec((tm, d), lambda i, j: (i, 0)),
            pl.BlockSpec((1, d), lambda i, j: (0, 0)),
            pl.BlockSpec((d, tn), lambda i, j: (0, j)),
        ],
        out_specs=pl.BlockSpec((tm, tn), lambda i, j: (i, j)),
        scratch_shapes=[pltpu.VMEM((tm, d), BF16)],
        compiler_params=_params("parallel", "arbitrary"),
        name="proj_in",
    )(x2, gain.reshape(1, d), w)


def _hgrn_tables(c):
    t = np.arange(c)[:, None]
    r = np.arange(c)[None, :]
    sums, masks = [], []
    h = c // 2
    while h >= 1:
        start = (t // h) * h
        upper = (t // h) % 2 == 1
        sums.append(np.where(upper, (r >= start) & (r <= t), (r > t) & (r <= start + h - 1)))
        masks.append((t // (2 * h) == r // (2 * h)) & upper & ((r // h) % 2 == 0))
        h //= 2
    sums.append(r <= t)
    sums.append(r > t)
    return np.stack(sums).astype(np.float32), np.stack(masks).astype(np.float32)


def _hgrn_constants(c):
    sums, masks = _hgrn_tables(c)
    out = []
    for flip in (False, True):
        s = sums[:, ::-1, ::-1] if flip else sums
        m = masks[:, ::-1, ::-1] if flip else masks
        s = s.reshape(-1, c)
        out.append(jnp.asarray(np.concatenate([s, s], axis=1), dtype=BF16))
        out.append(jnp.asarray(m, dtype=F32))
    return out


def _hgrn_chunk(q, v, z, lb, sums_ref, masks_ref, state_ref, total_row):
    c = q.shape[0]
    n_levels = masks_ref.shape[0]
    sig = jax.nn.sigmoid(z)
    f = lb + (1.0 - lb) * sig
    k = (1.0 - lb) * (1.0 - sig)
    lf = jnp.log(f)
    hi = lf.astype(BF16)
    lo = (lf - hi.astype(F32)).astype(BF16)
    x = jnp.dot(sums_ref[...], jnp.concatenate([hi, lo], axis=0), preferred_element_type=F32)
    e = jnp.exp(x)
    outs = []
    for a in range(q.shape[1] // HEAD_DIM):
        sl = slice(a * HEAD_DIM, (a + 1) * HEAD_DIM)
        qa, ka, va = q[:, sl], k[:, sl], v[:, sl]
        scores = jnp.zeros((c, c), F32)
        for lvl in range(n_levels):
            el = e[lvl * c:(lvl + 1) * c, sl]
            s_l = lax.dot_general((qa * el).astype(BF16), (ka * el).astype(BF16), _NT,
                                  preferred_element_type=F32)
            scores = scores + s_l * masks_ref[lvl]
        e_q = e[n_levels * c:(n_levels + 1) * c, sl]
        e_k = e[(n_levels + 1) * c:(n_levels + 2) * c, sl]
        state_t = state_ref[a]
        va16 = va.astype(BF16)
        o = jnp.dot(scores.astype(BF16), va16, preferred_element_type=F32)
        o = o + jnp.sum(qa * ka, axis=-1, keepdims=True) * va
        o = o + lax.dot_general((qa * e_q).astype(BF16), state_t.astype(BF16), _NT,
                                preferred_element_type=F32)
        state_ref[a] = state_t * e_q[total_row:total_row + 1, :] + jnp.dot(
            va.T.astype(BF16), (ka * e_k).astype(BF16), preferred_element_type=F32)
        outs.append(o)
    return jnp.concatenate(outs, axis=-1)


def _hgrn_body(q_ref, v_ref, zf_ref, zb_ref, g_ref, lb_ref, gain_ref,
               sums_f_ref, masks_f_ref, sums_b_ref, masks_b_ref,
               o_ref, of_ref, ob_ref, state_f_ref, state_b_ref):
    c = masks_f_ref.shape[1]
    n_chunks = q_ref.shape[1] // c
    state_f_ref[...] = jnp.zeros_like(state_f_ref)
    state_b_ref[...] = jnp.zeros_like(state_b_ref)

    def scan_step(n, carry):
        rf = pl.multiple_of(n * c, c)
        rb = pl.multiple_of((n_chunks - 1 - n) * c, c)
        of_ref[pl.ds(rf, c), :] = _hgrn_chunk(
            q_ref[0, pl.ds(rf, c), :], v_ref[0, pl.ds(rf, c), :], zf_ref[0, pl.ds(rf, c), :],
            lb_ref[0:1, :], sums_f_ref, masks_f_ref, state_f_ref, c - 1)
        ob_ref[pl.ds(rb, c), :] = _hgrn_chunk(
            q_ref[0, pl.ds(rb, c), :], v_ref[0, pl.ds(rb, c), :], zb_ref[0, pl.ds(rb, c), :],
            lb_ref[1:2, :], sums_b_ref, masks_b_ref, state_b_ref, 0)
        return carry

    lax.fori_loop(0, n_chunks, scan_step, 0)

    def finish_step(n, carry):
        r = pl.multiple_of(n * c, c)
        o = of_ref[pl.ds(r, c), :] + ob_ref[pl.ds(r, c), :]
        g = g_ref[0, pl.ds(r, c), :]
        normed = [
            _rms_norm_rows(o[:, a * HEAD_DIM:(a + 1) * HEAD_DIM],
                           gain_ref[:, a * HEAD_DIM:(a + 1) * HEAD_DIM])
            for a in range(o.shape[1] // HEAD_DIM)
        ]
        y = jnp.concatenate(normed, axis=-1) * (g * jax.nn.sigmoid(g))
        o_ref[0, pl.ds(r, c), :] = y.astype(o_ref.dtype)
        return carry

    lax.fori_loop(0, n_chunks, finish_step, 0)


def _hgrn(p3, lb, gain, width):
    b, s, _ = p3.shape
    heads = width // HEAD_DIM
    nh = 2 if heads % 2 == 0 else 1
    wb = nh * HEAD_DIM
    n_blk = width // wb
    c = min(HGRN_CHUNK, s)
    sums_f, masks_f, sums_b, masks_b = _hgrn_constants(c)

    def sect(k):
        return pl.BlockSpec((1, s, wb), lambda bi, hi, k=k: (bi, 0, k * n_blk + hi))

    def whole(arr):
        return pl.BlockSpec(arr.shape, lambda bi, hi, nd=arr.ndim: (0,) * nd)

    return pl.pallas_call(
        _hgrn_body,
        out_shape=jax.ShapeDtypeStruct((b, s, width), BF16),
        grid=(b, n_blk),
        in_specs=[
            sect(0), sect(1), sect(2), sect(3), sect(4),
            pl.BlockSpec((2, wb), lambda bi, hi: (0, hi)),
            pl.BlockSpec((1, wb), lambda bi, hi: (0, hi)),
            whole(sums_f), whole(masks_f), whole(sums_b), whole(masks_b),
        ],
        out_specs=pl.BlockSpec((1, s, wb), lambda bi, hi: (bi, 0, hi)),
        scratch_shapes=[
            pltpu.VMEM((s, wb), F32), pltpu.VMEM((s, wb), F32),
            pltpu.VMEM((nh, HEAD_DIM, HEAD_DIM), F32), pltpu.VMEM((nh, HEAD_DIM, HEAD_DIM), F32),
        ],
        compiler_params=_params("parallel", "parallel"),
        name="hgrn",
    )(p3, p3, p3, p3, p3, lb, gain.reshape(1, width), sums_f, masks_f, sums_b, masks_b)


def _attn_body(lam_ref, q_ref, k_ref, v_ref, cos_ref, sin_ref, qg_ref, kg_ref, og_ref,
               o_ref, q1_ref, q2_ref, kb_ref, vb_ref, *, out_scale):
    s = q_ref.shape[1]
    lane = lax.broadcasted_iota(jnp.int32, (1, HEAD_DIM), 1)
    first_map = lane < ATT_MAP_DIM
    first_half = (lane % ATT_MAP_DIM) < (ATT_MAP_DIM // 2)
    cos = cos_ref[0]
    sin = sin_ref[0]

    def prep(t, gain):
        sq = t * t
        s1 = jnp.sum(jnp.where(first_map, sq, 0.0), axis=-1, keepdims=True)
        s2 = jnp.sum(jnp.where(first_map, 0.0, sq), axis=-1, keepdims=True)
        ms = jnp.where(first_map, s1, s2) * (1.0 / ATT_MAP_DIM)
        y = t * lax.rsqrt(ms + EPS) * gain
        half = ATT_MAP_DIM // 2
        swapped = jnp.where(first_half, pltpu.roll(y, HEAD_DIM - half, 1), pltpu.roll(y, half, 1))
        return y * cos + swapped * sin

    qr = prep(q_ref[0], qg_ref[...]) * (ATT_MAP_DIM ** -0.5)
    q1_ref[...] = jnp.where(first_map, qr, 0.0).astype(BF16)
    q2_ref[...] = jnp.where(first_map, 0.0, qr).astype(BF16)
    kb_ref[...] = prep(k_ref[0], kg_ref[...]).astype(BF16)
    vb_ref[...] = v_ref[0].astype(BF16)
    lam = lam_ref[0]
    tq = min(ATT_Q_BLOCK, s)

    def softmax_parts(qm_ref, r):
        sc = lax.dot_general(qm_ref[pl.ds(r, tq), :], kb_ref[...], _NT, preferred_element_type=F32)
        e = jnp.exp(sc - jnp.max(sc, axis=-1, keepdims=True))
        return e, jnp.sum(e, axis=-1, keepdims=True)

    def q_step(n, carry):
        r = pl.multiple_of(n * tq, tq)
        e1, l1 = softmax_parts(q1_ref, r)
        e2, l2 = softmax_parts(q2_ref, r)
        a = e1 * (1.0 / l1) - e2 * (lam / l2)
        o = jnp.dot(a.astype(BF16), vb_ref[...], preferred_element_type=F32)
        y = _rms_norm_rows(o, og_ref[...]) * out_scale
        o_ref[0, pl.ds(r, tq), :] = y.astype(o_ref.dtype)
        return carry

    lax.fori_loop(0, s // tq, q_step, 0)


def _attn(p3, cos_t, sin_t, lam, q_gain, k_gain, out_gain, col0, width, out_scale):
    b, s, _ = p3.shape
    heads = width // HEAD_DIM
    blk0 = col0 // HEAD_DIM

    def sect(k):
        return pl.BlockSpec((1, s, HEAD_DIM), lambda bi, hi, k=k: (bi, 0, blk0 + k * heads + hi))

    def tiled_gain(g):
        return jnp.tile(g, HEAD_DIM // ATT_MAP_DIM).reshape(1, HEAD_DIM)

    row = pl.BlockSpec((1, HEAD_DIM), lambda bi, hi: (0, 0))
    table = pl.BlockSpec((1, s, HEAD_DIM), lambda bi, hi: (bi, 0, 0))
    return pl.pallas_call(
        functools.partial(_attn_body, out_scale=out_scale),
        out_shape=jax.ShapeDtypeStruct((b, s, width), BF16),
        grid=(b, heads),
        in_specs=[
            pl.BlockSpec(memory_space=pltpu.SMEM),
            sect(0), sect(1), sect(2), table, table, row, row,
            pl.BlockSpec((1, HEAD_DIM), lambda bi, hi: (0, hi)),
        ],
        out_specs=pl.BlockSpec((1, s, HEAD_DIM), lambda bi, hi: (bi, 0, hi)),
        scratch_shapes=[pltpu.VMEM((s, HEAD_DIM), BF16) for _ in range(4)],
        compiler_params=_params("parallel", "parallel"),
        name="attn",
    )(lam.reshape(1), p3, p3, p3, cos_t, sin_t, tiled_gain(q_gain), tiled_gain(k_gain),
      out_gain.reshape(1, width))


def _conv_body(b_ref, c_ref, u_ref, w_ref, bias_ref, gain_ref, o_ref):
    s = u_ref.shape[1]
    v = c_ref[0] * u_ref[0]
    row = lax.broadcasted_iota(jnp.int32, (s, 1), 0)
    prev = jnp.where(row == 0, 0.0, pltpu.roll(v, 1, 0))
    nxt = jnp.where(row == s - 1, 0.0, pltpu.roll(v, s - 1, 0))
    y = bias_ref[...] + prev * w_ref[0:1, :]
    y = y + v * w_ref[1:2, :]
    y = y + nxt * w_ref[2:3, :]
    y = b_ref[0] * y
    o_ref[0] = _rms_norm_rows(y, gain_ref[...]).astype(o_ref.dtype)


def _conv(p3, w, bias, gain, col0, width):
    b, s, _ = p3.shape
    groups = width // HEAD_DIM
    blk0 = col0 // HEAD_DIM

    def sect(k):
        return pl.BlockSpec((1, s, HEAD_DIM), lambda bi, gi, k=k: (bi, 0, blk0 + k * groups + gi))

    return pl.pallas_call(
        _conv_body,
        out_shape=jax.ShapeDtypeStruct((b, s, width), BF16),
        grid=(b, groups),
        in_specs=[
            sect(0), sect(1), sect(2),
            pl.BlockSpec((w.shape[0], HEAD_DIM), lambda bi, gi: (0, gi)),
            pl.BlockSpec((1, HEAD_DIM), lambda bi, gi: (0, gi)),
            pl.BlockSpec((1, HEAD_DIM), lambda bi, gi: (0, gi)),
        ],
        out_specs=pl.BlockSpec((1, s, HEAD_DIM), lambda bi, gi: (bi, 0, gi)),
        compiler_params=_params("parallel", "parallel"),
        name="conv",
    )(p3, p3, p3, w, bias.reshape(1, width), gain.reshape(1, width))


def _proj_out_body(x_ref, ya_ref, yb_ref, yc_ref, wa_ref, wb_ref, wc_ref, o_ref):
    acc = jnp.dot(ya_ref[...], wa_ref[...], preferred_element_type=F32)
    acc = acc + jnp.dot(yb_ref[...], wb_ref[...], preferred_element_type=F32)
    acc = acc + jnp.dot(yc_ref[...], wc_ref[...], preferred_element_type=F32)
    o_ref[...] = x_ref[...] + acc


def _proj_out(x2, ya, yb, yc, w):
    t, d = x2.shape
    wa, wb, wc = ya.shape[1], yb.shape[1], yc.shape[1]
    tm = _tile(t, 512, 8)

    def rows(width):
        return pl.BlockSpec((tm, width), lambda i: (i, 0))

    def whole(n_rows):
        return pl.BlockSpec((n_rows, d), lambda i: (0, 0))

    return pl.pallas_call(
        _proj_out_body,
        out_shape=jax.ShapeDtypeStruct((t, d), F32),
        grid=(t // tm,),
        in_specs=[rows(d), rows(wa), rows(wb), rows(wc), whole(wa), whole(wb), whole(wc)],
        out_specs=rows(d),
        compiler_params=_params("parallel"),
        name="proj_out",
    )(x2, ya, yb, yc, w[:wa], w[wa:wa + wb], w[wa + wb:])


def _pad_ffn_weights(wg, wu, wd):
    f = wg.shape[1]
    fp = -(-f // 512) * 512 if f > 512 else f
    pad = fp - f
    wg = jnp.pad(wg.astype(BF16), ((0, 0), (0, pad)))
    wu = jnp.pad(wu.astype(BF16), ((0, 0), (0, pad)))
    wd = jnp.pad(wd.astype(BF16), ((0, pad), (0, 0)))
    return wg, wu, wd


def kernel(x, positions, ffn1_norm, ffn1_w_gate, ffn1_w_up, ffn1_w_down, mix_norm, w_in, hgrn_lb_logits, hgrn_norm, da_q_norm, da_k_norm, da_lambda_q1, da_lambda_k1, da_lambda_q2, da_lambda_k2, da_out_norm, conv_w, conv_b, conv_norm, w_out, ffn2_norm, ffn2_w_gate, ffn2_w_up, ffn2_w_down):
    b, s, d = x.shape
    depth = w_in.shape[0]
    hg_width = d // 2
    da_width = d // 4
    sc_width = d - hg_width - da_width
    da_col0 = 5 * hg_width
    sc_col0 = da_col0 + 3 * da_width

    half = ATT_MAP_DIM // 2
    inv_freq = ROPE_THETA ** (-jnp.arange(0, ATT_MAP_DIM, 2, dtype=F32) / ATT_MAP_DIM)
    ang = positions.astype(F32)[..., None] * inv_freq
    cos_t = jnp.tile(jnp.cos(ang), (1, 1, HEAD_DIM // half))
    sin_half = jnp.sin(ang)
    sin_t = jnp.tile(jnp.concatenate([-sin_half, sin_half], axis=-1), (1, 1, HEAD_DIM // ATT_MAP_DIM))

    lb_all = jnp.cumsum(jax.nn.softmax(hgrn_lb_logits.astype(F32), axis=1), axis=1)
    lb_all = lb_all - lb_all[:, :1]

    x2 = x.reshape(b * s, d)
    for layer in range(depth):
        wg, wu, wd = _pad_ffn_weights(ffn1_w_gate[layer], ffn1_w_up[layer], ffn1_w_down[layer])
        x2 = _ffn(x2, ffn1_norm[layer], wg, wu, wd)

        p3 = _proj_in(x2, mix_norm[layer], w_in[layer].astype(BF16)).reshape(b, s, -1)

        y_a = _hgrn(p3, lb_all[:, layer], hgrn_norm[layer], hg_width)

        lam_init = 0.8 - 0.6 * math.exp(-0.3 * layer)
        lam = (jnp.exp(jnp.sum(da_lambda_q1[layer].astype(F32) * da_lambda_k1[layer].astype(F32)))
               - jnp.exp(jnp.sum(da_lambda_q2[layer].astype(F32) * da_lambda_k2[layer].astype(F32)))
               + lam_init)
        y_b = _attn(p3, cos_t, sin_t, lam, da_q_norm[layer], da_k_norm[layer], da_out_norm[layer],
                    da_col0, da_width, 1.0 - lam_init)

        y_c = _conv(p3, conv_w[layer], conv_b[layer], conv_norm[layer], sc_col0, sc_width)

        x2 = _proj_out(x2, y_a.reshape(b * s, -1), y_b.reshape(b * s, -1), y_c.reshape(b * s, -1),
                       w_out[layer].astype(BF16))

        wg, wu, wd = _pad_ffn_weights(ffn2_w_gate[layer], ffn2_w_up[layer], ffn2_w_down[layer])
        x2 = _ffn(x2, ffn2_norm[layer], wg, wu, wd)
    return x2.reshape(b, s, d)
```

```python
import functools
import math

import jax
import jax.numpy as jnp
import numpy as np
from jax import lax
from jax.experimental import pallas as pl
from jax.experimental.pallas import tpu as pltpu

F32 = jnp.float32
BF16 = jnp.bfloat16

EPS = 1e-6
ROPE_THETA = 10000.0
LANES = 128
HEAD_DIM = LANES
ATT_MAP_DIM = 64
HGRN_CHUNK = 128
ATT_Q_BLOCK = 256
VMEM_LIMIT_BYTES = 56 * 1024 * 1024

_NT = (((1,), (1,)), ((), ()))


def _rms_norm_rows(x, gain):
    ms = jnp.mean(x * x, axis=-1, keepdims=True)
    return x * lax.rsqrt(ms + EPS) * gain


def _tile(n, pref, quantum):
    if n <= pref:
        return n
    t = (pref // quantum) * quantum
    while n % t:
        t -= quantum
    return t


def _params(*semantics):
    return pltpu.CompilerParams(dimension_semantics=semantics, vmem_limit_bytes=VMEM_LIMIT_BYTES)


def _ffn_body(x_ref, gain_ref, wg_ref, wu_ref, wd_ref, o_ref, h_ref):
    j = pl.program_id(1)

    @pl.when(j == 0)
    def _():
        x = x_ref[...]
        h_ref[...] = _rms_norm_rows(x, gain_ref[...]).astype(BF16)
        o_ref[...] = x

    h = h_ref[...]
    g = jnp.dot(h, wg_ref[...], preferred_element_type=F32)
    u = jnp.dot(h, wu_ref[...], preferred_element_type=F32)
    a = (g * jax.nn.sigmoid(g) * (0.5 * u)).astype(BF16)
    o_ref[...] += jnp.dot(a, wd_ref[...], preferred_element_type=F32)


def _ffn(x2, gain, wg, wu, wd):
    t, d = x2.shape
    fp = wg.shape[1]
    tm = _tile(t, 1024, 8)
    tf = _tile(fp, 512, LANES)
    return pl.pallas_call(
        _ffn_body,
        out_shape=jax.ShapeDtypeStruct((t, d), F32),
        grid=(t // tm, fp // tf),
        in_specs=[
            pl.BlockSpec((tm, d), lambda i, j: (i, 0)),
            pl.BlockSpec((1, d), lambda i, j: (0, 0)),
            pl.BlockSpec((d, tf), lambda i, j: (0, j)),
            pl.BlockSpec((d, tf), lambda i, j: (0, j)),
            pl.BlockSpec((tf, d), lambda i, j: (j, 0)),
        ],
        out_specs=pl.BlockSpec((tm, d), lambda i, j: (i, 0)),
        scratch_shapes=[pltpu.VMEM((tm, d), BF16)],
        compiler_params=_params("parallel", "arbitrary"),
        name="ffn",
    )(x2, gain.reshape(1, d), wg, wu, wd)


def _proj_in_body(x_ref, gain_ref, w_ref, o_ref, h_ref):
    @pl.when(pl.program_id(1) == 0)
    def _():
        h_ref[...] = _rms_norm_rows(x_ref[...], gain_ref[...]).astype(BF16)

    o_ref[...] = jnp.dot(h_ref[...], w_ref[...], preferred_element_type=F32)


def _proj_in(x2, gain, w):
    t, d = x2.shape
    n = w.shape[1]
    tm = _tile(t, 1024, 8)
    tn = _tile(n, 1024, LANES)
    return pl.pallas_call(
        _proj_in_body,
        out_shape=jax.ShapeDtypeStruct((t, n), F32),
        grid=(t // tm, n // tn),
        in_specs=[
            pl.BlockSpec((tm, d), lambda i, j: (i, 0)),
            pl.BlockSpec((1, d), lambda i, j: (0, 0)),
            pl.BlockSpec((d, tn), lambda i, j: (0, j)),
        ],
        out_specs=pl.BlockSpec((tm, tn), lambda i, j: (i, j)),
        scratch_shapes=[pltpu.VMEM((tm, d), BF16)],
        compiler_params=_params("parallel", "arbitrary"),
        name="proj_in",
    )(x2, gain.reshape(1, d), w)


def _hgrn_tables(c):
    t = np.arange(c)[:, None]
    r = np.arange(c)[None, :]
    sums, masks = [], []
    h = c // 2
    while h >= 1:
        start = (t // h) * h
        upper = (t // h) % 2 == 1
        sums.append(np.where(upper, (r >= start) & (r <= t), (r > t) & (r <= start + h - 1)))
        masks.append((t // (2 * h) == r // (2 * h)) & upper & ((r // h) % 2 == 0))
        h //= 2
    sums.append(r <= t)
    sums.append(r > t)
    return np.stack(sums).astype(np.float32), np.stack(masks).astype(np.float32)


def _hgrn_constants(c):
    sums, masks = _hgrn_tables(c)
    out = []
    for flip in (False, True):
        s = sums[:, ::-1, ::-1] if flip else sums
        m = masks[:, ::-1, ::-1] if flip else masks
        s = s.reshape(-1, c)
        out.append(jnp.asarray(np.concatenate([s, s], axis=1), dtype=BF16))
        out.append(jnp.asarray(m, dtype=F32))
    return out


def _hgrn_chunk(q, v, z, lb, sums_ref, masks_ref, state_ref, total_row):
    c = q.shape[0]
    n_levels = masks_ref.shape[0]
    sig = jax.nn.sigmoid(z)
    f = lb + (1.0 - lb) * sig
    k = (1.0 - lb) * (1.0 - sig)
    lf = jnp.log2(f)
    hi = lf.astype(BF16)
    lo = (lf - hi.astype(F32)).astype(BF16)
    x = jnp.dot(sums_ref[...], jnp.concatenate([hi, lo], axis=0), preferred_element_type=F32)
    e = jnp.exp2(x)
    outs = []
    for a in range(q.shape[1] // HEAD_DIM):
        sl = slice(a * HEAD_DIM, (a + 1) * HEAD_DIM)
        qa, ka, va = q[:, sl], k[:, sl], v[:, sl]
        scores = jnp.zeros((c, c), F32)
        for lvl in range(n_levels):
            el = e[lvl * c:(lvl + 1) * c, sl]
            s_l = lax.dot_general((qa * el).astype(BF16), (ka * el).astype(BF16), _NT,
                                  preferred_element_type=F32)
            scores = scores + s_l * masks_ref[lvl]
        e_q = e[n_levels * c:(n_levels + 1) * c, sl]
        e_k = e[(n_levels + 1) * c:(n_levels + 2) * c, sl]
        state_t = state_ref[a]
        va16 = va.astype(BF16)
        o = jnp.dot(scores.astype(BF16), va16, preferred_element_type=F32)
        o = o + jnp.sum(qa * ka, axis=-1, keepdims=True) * va
        o = o + lax.dot_general((qa * e_q).astype(BF16), state_t.astype(BF16), _NT,
                                preferred_element_type=F32)
        state_ref[a] = state_t * e_q[total_row:total_row + 1, :] + jnp.dot(
            va.T.astype(BF16), (ka * e_k).astype(BF16), preferred_element_type=F32)
        outs.append(o)
    return jnp.concatenate(outs, axis=-1)


def _hgrn_body(q_ref, v_ref, zf_ref, zb_ref, g_ref, lb_ref, gain_ref,
               sums_f_ref, masks_f_ref, sums_b_ref, masks_b_ref,
               o_ref, of_ref, ob_ref, state_f_ref, state_b_ref):
    c = masks_f_ref.shape[1]
    n_chunks = q_ref.shape[1] // c
    state_f_ref[...] = jnp.zeros_like(state_f_ref)
    state_b_ref[...] = jnp.zeros_like(state_b_ref)

    def scan_step(n, carry):
        rf = pl.multiple_of(n * c, c)
        rb = pl.multiple_of((n_chunks - 1 - n) * c, c)
        of_ref[pl.ds(rf, c), :] = _hgrn_chunk(
            q_ref[0, pl.ds(rf, c), :], v_ref[0, pl.ds(rf, c), :], zf_ref[0, pl.ds(rf, c), :],
            lb_ref[0:1, :], sums_f_ref, masks_f_ref, state_f_ref, c - 1)
        ob_ref[pl.ds(rb, c), :] = _hgrn_chunk(
            q_ref[0, pl.ds(rb, c), :], v_ref[0, pl.ds(rb, c), :], zb_ref[0, pl.ds(rb, c), :],
            lb_ref[1:2, :], sums_b_ref, masks_b_ref, state_b_ref, 0)
        return carry

    lax.fori_loop(0, n_chunks, scan_step, 0, unroll=2)

    def finish_step(n, carry):
        r = pl.multiple_of(n * c, c)
        o = of_ref[pl.ds(r, c), :] + ob_ref[pl.ds(r, c), :]
        g = g_ref[0, pl.ds(r, c), :]
        normed = [
            _rms_norm_rows(o[:, a * HEAD_DIM:(a + 1) * HEAD_DIM],
                           gain_ref[:, a * HEAD_DIM:(a + 1) * HEAD_DIM])
            for a in range(o.shape[1] // HEAD_DIM)
        ]
        y = jnp.concatenate(normed, axis=-1) * (g * jax.nn.sigmoid(g))
        o_ref[0, pl.ds(r, c), :] = y.astype(o_ref.dtype)
        return carry

    lax.fori_loop(0, n_chunks, finish_step, 0)


def _hgrn(p3, lb, gain, width):
    b, s, _ = p3.shape
    heads = width // HEAD_DIM
    nh = 2 if heads % 2 == 0 else 1
    wb = nh * HEAD_DIM
    n_blk = width // wb
    c = min(HGRN_CHUNK, s)
    sums_f, masks_f, sums_b, masks_b = _hgrn_constants(c)

    def sect(k):
        return pl.BlockSpec((1, s, wb), lambda bi, hi, k=k: (bi, 0, k * n_blk + hi))

    def whole(arr):
        return pl.BlockSpec(arr.shape, lambda bi, hi, nd=arr.ndim: (0,) * nd)

    return pl.pallas_call(
        _hgrn_body,
        out_shape=jax.ShapeDtypeStruct((b, s, width), BF16),
        grid=(b, n_blk),
        in_specs=[
            sect(0), sect(1), sect(2), sect(3), sect(4),
            pl.BlockSpec((2, wb), lambda bi, hi: (0, hi)),
            pl.BlockSpec((1, wb), lambda bi, hi: (0, hi)),
            whole(sums_f), whole(masks_f), whole(sums_b), whole(masks_b),
        ],
        out_specs=pl.BlockSpec((1, s, wb), lambda bi, hi: (bi, 0, hi)),
        scratch_shapes=[
            pltpu.VMEM((s, wb), F32), pltpu.VMEM((s, wb), F32),
            pltpu.VMEM((nh, HEAD_DIM, HEAD_DIM), F32), pltpu.VMEM((nh, HEAD_DIM, HEAD_DIM), F32),
        ],
        compiler_params=_params("parallel", "parallel"),
        name="hgrn",
    )(p3, p3, p3, p3, p3, lb, gain.reshape(1, width), sums_f, masks_f, sums_b, masks_b)


def _attn_body(lam_ref, q_ref, k_ref, v_ref, cos_ref, sin_ref, qg_ref, kg_ref, og_ref,
               o_ref, q1_ref, q2_ref, kb_ref, vb_ref, *, out_scale):
    s = q_ref.shape[1]
    quarter = ATT_MAP_DIM // 2

    def lane_map(shape, axis):
        return (lax.broadcasted_iota(jnp.int32, shape, axis) // quarter) % 2

    first_map = lane_map((1, HEAD_DIM), 1) == 0
    square = (HEAD_DIM, HEAD_DIM)
    same_map = jnp.where(lane_map(square, 0) == lane_map(square, 1), 1.0, 0.0).astype(BF16)
    cos = cos_ref[0]
    sin = sin_ref[0]

    def prep(t, gain):
        sq = t * t
        hi = sq.astype(BF16)
        lo = (sq - hi.astype(F32)).astype(BF16)
        ms = (jnp.dot(hi, same_map, preferred_element_type=F32)
              + jnp.dot(lo, same_map, preferred_element_type=F32)) * (1.0 / ATT_MAP_DIM)
        y = t * lax.rsqrt(ms + EPS) * gain
        return y * cos + pltpu.roll(y, HEAD_DIM // 2, 1) * sin

    qr = prep(q_ref[0], qg_ref[...]) * (ATT_MAP_DIM ** -0.5 * math.log2(math.e))
    q1_ref[...] = jnp.where(first_map, qr, 0.0).astype(BF16)
    q2_ref[...] = jnp.where(first_map, 0.0, qr).astype(BF16)
    kb_ref[...] = prep(k_ref[0], kg_ref[...]).astype(BF16)
    vb_ref[...] = v_ref[0].astype(BF16)
    lam = lam_ref[0]
    tq = min(ATT_Q_BLOCK, s)

    def q_step(n, carry):
        r = pl.multiple_of(n * tq, tq)
        e, inv = [], []
        for qm_ref in (q1_ref, q2_ref):
            sc = lax.dot_general(qm_ref[pl.ds(r, tq), :], kb_ref[...], _NT, preferred_element_type=F32)
            em = jnp.exp2(sc - jnp.max(sc, axis=-1, keepdims=True))
            e.append(em.astype(BF16))
            inv.append(1.0 / jnp.sum(em, axis=-1, keepdims=True))
        a = e[0] * inv[0].astype(BF16) - e[1] * (lam * inv[1]).astype(BF16)
        o = jnp.dot(a, vb_ref[...], preferred_element_type=F32)
        y = _rms_norm_rows(o, og_ref[...]) * out_scale
        o_ref[0, pl.ds(r, tq), :] = y.astype(o_ref.dtype)
        return carry

    lax.fori_loop(0, s // tq, q_step, 0)


def _attn(p3, cos_t, sin_t, lam, q_gain, k_gain, out_gain, col0, width, out_scale):
    b, s, _ = p3.shape
    heads = width // HEAD_DIM
    blk0 = col0 // HEAD_DIM

    def sect(k):
        return pl.BlockSpec((1, s, HEAD_DIM), lambda bi, hi, k=k: (bi, 0, blk0 + k * heads + hi))

    def tiled_gain(g):
        half = ATT_MAP_DIM // 2
        return jnp.concatenate([g[:half], g[:half], g[half:], g[half:]]).reshape(1, HEAD_DIM)

    row = pl.BlockSpec((1, HEAD_DIM), lambda bi, hi: (0, 0))
    table = pl.BlockSpec((1, s, HEAD_DIM), lambda bi, hi: (bi, 0, 0))
    return pl.pallas_call(
        functools.partial(_attn_body, out_scale=out_scale),
        out_shape=jax.ShapeDtypeStruct((b, s, width), BF16),
        grid=(b, heads),
        in_specs=[
            pl.BlockSpec(memory_space=pltpu.SMEM),
            sect(0), sect(1), sect(2), table, table, row, row,
            pl.BlockSpec((1, HEAD_DIM), lambda bi, hi: (0, hi)),
        ],
        out_specs=pl.BlockSpec((1, s, HEAD_DIM), lambda bi, hi: (bi, 0, hi)),
        scratch_shapes=[pltpu.VMEM((s, HEAD_DIM), BF16) for _ in range(4)],
        compiler_params=_params("parallel", "parallel"),
        name="attn",
    )(lam.reshape(1), p3, p3, p3, cos_t, sin_t, tiled_gain(q_gain), tiled_gain(k_gain),
      out_gain.reshape(1, width))


def _conv_body(b_ref, c_ref, u_ref, w_ref, bias_ref, gain_ref, o_ref):
    s = u_ref.shape[1]
    v = c_ref[0] * u_ref[0]
    row = lax.broadcasted_iota(jnp.int32, (s, 1), 0)
    prev = jnp.where(row == 0, 0.0, pltpu.roll(v, 1, 0))
    nxt = jnp.where(row == s - 1, 0.0, pltpu.roll(v, s - 1, 0))
    y = bias_ref[...] + prev * w_ref[0:1, :]
    y = y + v * w_ref[1:2, :]
    y = y + nxt * w_ref[2:3, :]
    y = b_ref[0] * y
    o_ref[0] = _rms_norm_rows(y, gain_ref[...]).astype(o_ref.dtype)


def _conv(p3, w, bias, gain, col0, width):
    b, s, _ = p3.shape
    groups = width // HEAD_DIM
    blk0 = col0 // HEAD_DIM

    def sect(k):
        return pl.BlockSpec((1, s, HEAD_DIM), lambda bi, gi, k=k: (bi, 0, blk0 + k * groups + gi))

    return pl.pallas_call(
        _conv_body,
        out_shape=jax.ShapeDtypeStruct((b, s, width), BF16),
        grid=(b, groups),
        in_specs=[
            sect(0), sect(1), sect(2),
            pl.BlockSpec((w.shape[0], HEAD_DIM), lambda bi, gi: (0, gi)),
            pl.BlockSpec((1, HEAD_DIM), lambda bi, gi: (0, gi)),
            pl.BlockSpec((1, HEAD_DIM), lambda bi, gi: (0, gi)),
        ],
        out_specs=pl.BlockSpec((1, s, HEAD_DIM), lambda bi, gi: (bi, 0, gi)),
        compiler_params=_params("parallel", "parallel"),
        name="conv",
    )(p3, p3, p3, w, bias.reshape(1, width), gain.reshape(1, width))


def _proj_out_body(x_ref, ya_ref, yb_ref, yc_ref, wa_ref, wb_ref, wc_ref, o_ref):
    acc = jnp.dot(ya_ref[...], wa_ref[...], preferred_element_type=F32)
    acc = acc + jnp.dot(yb_ref[...], wb_ref[...], preferred_element_type=F32)
    acc = acc + jnp.dot(yc_ref[...], wc_ref[...], preferred_element_type=F32)
    o_ref[...] = x_ref[...] + acc


def _proj_out(x2, ya, yb, yc, w):
    t, d = x2.shape
    wa, wb, wc = ya.shape[1], yb.shape[1], yc.shape[1]
    tm = _tile(t, 512, 8)

    def rows(width):
        return pl.BlockSpec((tm, width), lambda i: (i, 0))

    def whole(n_rows):
        return pl.BlockSpec((n_rows, d), lambda i: (0, 0))

    return pl.pallas_call(
        _proj_out_body,
        out_shape=jax.ShapeDtypeStruct((t, d), F32),
        grid=(t // tm,),
        in_specs=[rows(d), rows(wa), rows(wb), rows(wc), whole(wa), whole(wb), whole(wc)],
        out_specs=rows(d),
        compiler_params=_params("parallel"),
        name="proj_out",
    )(x2, ya, yb, yc, w[:wa], w[wa:wa + wb], w[wa + wb:])


def _pad_ffn_weights(wg, wu, wd):
    f = wg.shape[1]
    fp = -(-f // 512) * 512 if f > 512 else f
    pad = fp - f
    wg = jnp.pad(wg.astype(BF16), ((0, 0), (0, pad)))
    wu = jnp.pad(wu.astype(BF16), ((0, 0), (0, pad)))
    wd = jnp.pad(wd.astype(BF16), ((0, pad), (0, 0)))
    return wg, wu, wd


def _attn_lane_order(w, col0, width):
    d = w.shape[0]
    half = ATT_MAP_DIM // 2
    qk = w[:, col0:col0 + 2 * width].reshape(d, 2 * width // HEAD_DIM, 2, 2, half)
    qk = qk.transpose(0, 1, 3, 2, 4).reshape(d, 2 * width)
    return jnp.concatenate([w[:, :col0], qk, w[:, col0 + 2 * width:]], axis=1)


def kernel(x, positions, ffn1_norm, ffn1_w_gate, ffn1_w_up, ffn1_w_down, mix_norm, w_in, hgrn_lb_logits, hgrn_norm, da_q_norm, da_k_norm, da_lambda_q1, da_lambda_k1, da_lambda_q2, da_lambda_k2, da_out_norm, conv_w, conv_b, conv_norm, w_out, ffn2_norm, ffn2_w_gate, ffn2_w_up, ffn2_w_down):
    b, s, d = x.shape
    depth = w_in.shape[0]
    hg_width = d // 2
    da_width = d // 4
    sc_width = d - hg_width - da_width
    da_col0 = 5 * hg_width
    sc_col0 = da_col0 + 3 * da_width

    half = ATT_MAP_DIM // 2
    inv_freq = ROPE_THETA ** (-jnp.arange(0, ATT_MAP_DIM, 2, dtype=F32) / ATT_MAP_DIM)
    ang = positions.astype(F32)[..., None] * inv_freq
    cos_t = jnp.tile(jnp.cos(ang), (1, 1, HEAD_DIM // half))
    sin_half = jnp.sin(ang)
    sin_t = jnp.concatenate([-sin_half, -sin_half, sin_half, sin_half], axis=-1)

    lb_all = jnp.cumsum(jax.nn.softmax(hgrn_lb_logits.astype(F32), axis=1), axis=1)
    lb_all = lb_all - lb_all[:, :1]

    x2 = x.reshape(b * s, d)
    for layer in range(depth):
        wg, wu, wd = _pad_ffn_weights(ffn1_w_gate[layer], ffn1_w_up[layer], ffn1_w_down[layer])
        x2 = _ffn(x2, ffn1_norm[layer], wg, wu, wd)

        w_in_l = _attn_lane_order(w_in[layer], da_col0, da_width).astype(BF16)
        p3 = _proj_in(x2, mix_norm[layer], w_in_l).reshape(b, s, -1)

        y_a = _hgrn(p3, lb_all[:, layer], hgrn_norm[layer], hg_width)

        lam_init = 0.8 - 0.6 * math.exp(-0.3 * layer)
        lam = (jnp.exp(jnp.sum(da_lambda_q1[layer].astype(F32) * da_lambda_k1[layer].astype(F32)))
               - jnp.exp(jnp.sum(da_lambda_q2[layer].astype(F32) * da_lambda_k2[layer].astype(F32)))
               + lam_init)
        y_b = _attn(p3, cos_t, sin_t, lam, da_q_norm[layer], da_k_norm[layer], da_out_norm[layer],
                    da_col0, da_width, 1.0 - lam_init)

        y_c = _conv(p3, conv_w[layer], conv_b[layer], conv_norm[layer], sc_col0, sc_width)

        x2 = _proj_out(x2, y_a.reshape(b * s, -1), y_b.reshape(b * s, -1), y_c.reshape(b * s, -1),
                       w_out[layer].astype(BF16))

        wg, wu, wd = _pad_ffn_weights(ffn2_w_gate[layer], ffn2_w_up[layer], ffn2_w_down[layer])
        x2 = _ffn(x2, ffn2_norm[layer], wg, wu, wd)
    return x2.reshape(b, s, d)
```

```python
import functools
import math

import jax
import jax.numpy as jnp
import numpy as np
from jax import lax
from jax.experimental import pallas as pl
from jax.experimental.pallas import tpu as pltpu

F32 = jnp.float32
BF16 = jnp.bfloat16

EPS = 1e-6
ROPE_THETA = 10000.0
LANES = 128
HEAD_DIM = LANES
ATT_MAP_DIM = 64
HGRN_CHUNK = 128
ATT_Q_BLOCK = 256
FFN_TILE = 512
VMEM_LIMIT_BYTES = 56 * 1024 * 1024

_NT = (((1,), (1,)), ((), ()))


def _rms_norm_rows(x, gain):
    ms = jnp.mean(x * x, axis=-1, keepdims=True)
    return x * lax.rsqrt(ms + EPS) * gain


def _tile(n, pref, quantum):
    if n <= pref:
        return n
    t = (pref // quantum) * quantum
    while n % t:
        t -= quantum
    return t


def _params(*semantics):
    return pltpu.CompilerParams(dimension_semantics=semantics, vmem_limit_bytes=VMEM_LIMIT_BYTES)


def _ffn_body(x_ref, gain_ref, wg_ref, wu_ref, wd_ref, o_ref, h_ref):
    j = pl.program_id(1)

    @pl.when(j == 0)
    def _():
        x = x_ref[...]
        h_ref[...] = _rms_norm_rows(x, gain_ref[...]).astype(BF16)
        o_ref[...] = x

    h = h_ref[...]
    g = jnp.dot(h, wg_ref[...], preferred_element_type=F32)
    u = jnp.dot(h, wu_ref[...], preferred_element_type=F32)
    a = (g * jax.nn.sigmoid(g) * (0.5 * u)).astype(BF16)
    o_ref[...] += jnp.dot(a, wd_ref[...], preferred_element_type=F32)


def _ffn(x2, gain, wg, wu, wd):
    t, d = x2.shape
    fp = wg.shape[1]
    tm = _tile(t, 1024, 8)
    tf = _tile(fp, FFN_TILE, LANES)
    return pl.pallas_call(
        _ffn_body,
        out_shape=jax.ShapeDtypeStruct((t, d), F32),
        grid=(t // tm, fp // tf),
        in_specs=[
            pl.BlockSpec((tm, d), lambda i, j: (i, 0)),
            pl.BlockSpec((1, d), lambda i, j: (0, 0)),
            pl.BlockSpec((d, tf), lambda i, j: (0, j)),
            pl.BlockSpec((d, tf), lambda i, j: (0, j)),
            pl.BlockSpec((tf, d), lambda i, j: (j, 0)),
        ],
        out_specs=pl.BlockSpec((tm, d), lambda i, j: (i, 0)),
        scratch_shapes=[pltpu.VMEM((tm, d), BF16)],
        compiler_params=_params("parallel", "arbitrary"),
        name="ffn",
    )(x2, gain.reshape(1, d), wg, wu, wd)


def _proj_in_body(x_ref, gain_ref, w_ref, o_ref, h_ref):
    @pl.when(pl.program_id(1) == 0)
    def _():
        h_ref[...] = _rms_norm_rows(x_ref[...], gain_ref[...]).astype(BF16)

    o_ref[...] = jnp.dot(h_ref[...], w_ref[...], preferred_element_type=F32)


def _proj_in(x2, gain, w):
    t, d = x2.shape
    n = w.shape[1]
    tm = _tile(t, 1024, 8)
    tn = _tile(n, 1024, LANES)
    return pl.pallas_call(
        _proj_in_body,
        out_shape=jax.ShapeDtypeStruct((t, n), F32),
        grid=(t // tm, n // tn),
        in_specs=[
            pl.BlockSpec((tm, d), lambda i, j: (i, 0)),
            pl.BlockSpec((1, d), lambda i, j: (0, 0)),
            pl.BlockSpec((d, tn), lambda i, j: (0, j)),
        ],
        out_specs=pl.BlockSpec((tm, tn), lambda i, j: (i, j)),
        scratch_shapes=[pltpu.VMEM((tm, d), BF16)],
        compiler_params=_params("parallel", "arbitrary"),
        name="proj_in",
    )(x2, gain.reshape(1, d), w)


def _hgrn_constants(c):
    t = np.arange(c)[:, None]
    r = np.arange(c)[None, :]
    masks, small = [], []
    h = c // 2
    while h >= 1:
        start = (t // h) * h
        upper = (t // h) % 2 == 1
        masks.append((t // (2 * h) == r // (2 * h)) & upper & ((r // h) % 2 == 0))
        if h <= 2:
            small.append(np.where(upper, (r >= start) & (r <= t), (r > t) & (r <= start + h - 1)))
        h //= 2
    masks = np.stack(masks).astype(np.float32)
    sums = np.stack([r <= t] + small).astype(np.float32)
    out = []
    for flip in (False, True):
        m = masks[:, ::-1, ::-1] if flip else masks
        s = (sums[:, ::-1, ::-1] if flip else sums).reshape(-1, c)
        out.append(jnp.asarray(np.concatenate([s, s], axis=1), dtype=BF16))
        out.append(jnp.asarray(m, dtype=F32))
    return out


def _neg_abs(d):
    sign = jnp.int32(-2 ** 31)
    return lax.bitcast_convert_type(lax.bitcast_convert_type(d, jnp.int32) | sign, F32)


def _hgrn_level_exponents(cum, cum_ref, reverse):
    c = cum.shape[0]
    xs = []
    h = c // 2
    while h >= 4:
        pieces = []
        for j in range(c // (2 * h)):
            r0 = j * 2 * h
            b = cum_ref[pl.ds(r0 + h if reverse else r0 + h - 1, 1), :]
            if h >= 8:
                lo_rows, hi_rows = cum[r0:r0 + h], cum[r0 + h:r0 + 2 * h]
                pieces.extend((lo_rows - b, b - hi_rows) if reverse else (b - lo_rows, hi_rows - b))
            else:
                pieces.append(_neg_abs(cum[r0:r0 + 2 * h] - b))
        xs.append(jnp.concatenate(pieces, axis=0))
        h //= 2
    return xs


def _hgrn_chunk(q, v, z, lb, sums_ref, masks_ref, state_ref, cum_ref, reverse):
    c = q.shape[0]
    f = lb + (1.0 - lb) * jax.nn.sigmoid(z)
    k = 1.0 - f
    lf = jnp.log2(f)
    hi = lf.astype(BF16)
    lo = (lf - hi.astype(F32)).astype(BF16)
    sums = jnp.dot(sums_ref[...], jnp.concatenate([hi, lo], axis=0), preferred_element_type=F32)
    cum = sums[:c]
    cum_ref[...] = cum
    total_row = 0 if reverse else c - 1
    x_key = cum_ref[pl.ds(total_row, 1), :] - cum
    x_levels = _hgrn_level_exponents(cum, cum_ref, reverse) + [sums[c:2 * c], sums[2 * c:]]
    e_levels = [jnp.exp2(x).astype(BF16) for x in x_levels]
    e_q_all = jnp.exp2(cum)
    e_k_all = jnp.exp2(x_key).astype(BF16)
    q16, k16, v16 = q.astype(BF16), k.astype(BF16), v.astype(BF16)
    outs = []
    for a in range(q.shape[1] // HEAD_DIM):
        sl = slice(a * HEAD_DIM, (a + 1) * HEAD_DIM)
        qa, ka, va = q16[:, sl], k16[:, sl], v16[:, sl]
        ka_t = ka.T
        scores = jnp.zeros((c, c), F32)
        for lvl, e_l in enumerate(e_levels):
            el = e_l[:, sl]
            s_l = jnp.dot(qa * el, ka_t * el.T, preferred_element_type=F32)
            scores = scores + s_l * masks_ref[lvl]
        e_q = e_q_all[:, sl]
        state_t = state_ref[a]
        o = jnp.dot(scores.astype(BF16), va, preferred_element_type=F32)
        o = o + jnp.sum(q[:, sl] * k[:, sl], axis=-1, keepdims=True) * v[:, sl]
        o = o + lax.dot_general(qa * e_q.astype(BF16), state_t.astype(BF16), _NT,
                                preferred_element_type=F32)
        state_ref[a] = state_t * e_q[total_row:total_row + 1, :] + jnp.dot(
            v[:, sl].T.astype(BF16), ka * e_k_all[:, sl], preferred_element_type=F32)
        outs.append(o)
    return jnp.concatenate(outs, axis=-1)


def _hgrn_body(q_ref, v_ref, zf_ref, zb_ref, g_ref, lb_ref, gain_ref,
               sums_f_ref, masks_f_ref, sums_b_ref, masks_b_ref,
               o_ref, of_ref, ob_ref, state_f_ref, state_b_ref, cum_ref):
    c = masks_f_ref.shape[1]
    n_chunks = q_ref.shape[1] // c
    per_step = cum_ref.shape[0] // 2
    state_f_ref[...] = jnp.zeros_like(state_f_ref)
    state_b_ref[...] = jnp.zeros_like(state_b_ref)

    def scan_step(n, carry):
        for u in range(per_step):
            m = n * per_step + u
            rf = pl.multiple_of(m * c, c)
            rb = pl.multiple_of((n_chunks - 1 - m) * c, c)
            of_ref[pl.ds(rf, c), :] = _hgrn_chunk(
                q_ref[0, pl.ds(rf, c), :], v_ref[0, pl.ds(rf, c), :], zf_ref[0, pl.ds(rf, c), :],
                lb_ref[0:1, :], sums_f_ref, masks_f_ref, state_f_ref, cum_ref.at[2 * u], False)
            ob_ref[pl.ds(rb, c), :] = _hgrn_chunk(
                q_ref[0, pl.ds(rb, c), :], v_ref[0, pl.ds(rb, c), :], zb_ref[0, pl.ds(rb, c), :],
                lb_ref[1:2, :], sums_b_ref, masks_b_ref, state_b_ref, cum_ref.at[2 * u + 1], True)
        return carry

    lax.fori_loop(0, n_chunks // per_step, scan_step, 0)

    def finish_step(n, carry):
        r = pl.multiple_of(n * c, c)
        o = of_ref[pl.ds(r, c), :] + ob_ref[pl.ds(r, c), :]
        g = g_ref[0, pl.ds(r, c), :]
        normed = [
            _rms_norm_rows(o[:, a * HEAD_DIM:(a + 1) * HEAD_DIM],
                           gain_ref[:, a * HEAD_DIM:(a + 1) * HEAD_DIM])
            for a in range(o.shape[1] // HEAD_DIM)
        ]
        y = jnp.concatenate(normed, axis=-1) * (g * jax.nn.sigmoid(g))
        o_ref[0, pl.ds(r, c), :] = y.astype(o_ref.dtype)
        return carry

    lax.fori_loop(0, n_chunks, finish_step, 0)


def _hgrn(p3, lb, gain, width):
    b, s, _ = p3.shape
    heads = width // HEAD_DIM
    nh = 2 if heads % 2 == 0 else 1
    wb = nh * HEAD_DIM
    n_blk = width // wb
    c = min(HGRN_CHUNK, s)
    sums_f, masks_f, sums_b, masks_b = _hgrn_constants(c)
    per_step = 4 if (s // c) % 4 == 0 else 1

    def sect(k):
        return pl.BlockSpec((1, s, wb), lambda bi, hi, k=k: (bi, 0, k * n_blk + hi))

    def whole(arr):
        return pl.BlockSpec(arr.shape, lambda bi, hi, nd=arr.ndim: (0,) * nd)

    return pl.pallas_call(
        _hgrn_body,
        out_shape=jax.ShapeDtypeStruct((b, s, width), BF16),
        grid=(b, n_blk),
        in_specs=[
            sect(0), sect(1), sect(2), sect(3), sect(4),
            pl.BlockSpec((2, wb), lambda bi, hi: (0, hi)),
            pl.BlockSpec((1, wb), lambda bi, hi: (0, hi)),
            whole(sums_f), whole(masks_f), whole(sums_b), whole(masks_b),
        ],
        out_specs=pl.BlockSpec((1, s, wb), lambda bi, hi: (bi, 0, hi)),
        scratch_shapes=[
            pltpu.VMEM((s, wb), F32), pltpu.VMEM((s, wb), F32),
            pltpu.VMEM((nh, HEAD_DIM, HEAD_DIM), F32), pltpu.VMEM((nh, HEAD_DIM, HEAD_DIM), F32),
            pltpu.VMEM((2 * per_step, c, wb), F32),
        ],
        compiler_params=_params("parallel", "parallel"),
        name="hgrn",
    )(p3, p3, p3, p3, p3, lb, gain.reshape(1, width), sums_f, masks_f, sums_b, masks_b)


def _attn_body(lam_ref, q_ref, k_ref, v_ref, cos_ref, sin_ref, qg_ref, kg_ref, og_ref,
               o_ref, q1_ref, q2_ref, kb_ref, vb_ref, *, out_scale):
    s = q_ref.shape[1]
    quarter = ATT_MAP_DIM // 2

    def lane_map(shape, axis):
        return (lax.broadcasted_iota(jnp.int32, shape, axis) // quarter) % 2

    first_map = lane_map((1, HEAD_DIM), 1) == 0
    square = (HEAD_DIM, HEAD_DIM)
    same_map = jnp.where(lane_map(square, 0) == lane_map(square, 1), 1.0, 0.0).astype(BF16)
    cos = cos_ref[0]
    sin = sin_ref[0]

    def prep(t, gain):
        sq = t * t
        hi = sq.astype(BF16)
        lo = (sq - hi.astype(F32)).astype(BF16)
        ms = (jnp.dot(hi, same_map, preferred_element_type=F32)
              + jnp.dot(lo, same_map, preferred_element_type=F32)) * (1.0 / ATT_MAP_DIM)
        y = t * lax.rsqrt(ms + EPS) * gain
        return y * cos + pltpu.roll(y, HEAD_DIM // 2, 1) * sin

    qr = prep(q_ref[0], qg_ref[...]) * (ATT_MAP_DIM ** -0.5 * math.log2(math.e))
    q1_ref[...] = jnp.where(first_map, qr, 0.0).astype(BF16)
    q2_ref[...] = jnp.where(first_map, 0.0, qr).astype(BF16)
    kb_ref[...] = prep(k_ref[0], kg_ref[...]).astype(BF16)
    vb_ref[...] = v_ref[0].astype(BF16)
    lam = lam_ref[0]
    tq = min(ATT_Q_BLOCK, s)

    def q_step(n, carry):
        r = pl.multiple_of(n * tq, tq)
        e, inv = [], []
        for qm_ref in (q1_ref, q2_ref):
            sc = lax.dot_general(qm_ref[pl.ds(r, tq), :], kb_ref[...], _NT, preferred_element_type=F32)
            em = jnp.exp2(sc - jnp.max(sc, axis=-1, keepdims=True))
            e.append(em.astype(BF16))
            inv.append(1.0 / jnp.sum(em, axis=-1, keepdims=True))
        a = e[0] * inv[0].astype(BF16) - e[1] * (lam * inv[1]).astype(BF16)
        o = jnp.dot(a, vb_ref[...], preferred_element_type=F32)
        y = _rms_norm_rows(o, og_ref[...]) * out_scale
        o_ref[0, pl.ds(r, tq), :] = y.astype(o_ref.dtype)
        return carry

    lax.fori_loop(0, s // tq, q_step, 0)


def _attn(p3, cos_t, sin_t, lam, q_gain, k_gain, out_gain, col0, width, out_scale):
    b, s, _ = p3.shape
    heads = width // HEAD_DIM
    blk0 = col0 // HEAD_DIM

    def sect(k):
        return pl.BlockSpec((1, s, HEAD_DIM), lambda bi, hi, k=k: (bi, 0, blk0 + k * heads + hi))

    def tiled_gain(g):
        half = ATT_MAP_DIM // 2
        return jnp.concatenate([g[:half], g[:half], g[half:], g[half:]]).reshape(1, HEAD_DIM)

    row = pl.BlockSpec((1, HEAD_DIM), lambda bi, hi: (0, 0))
    table = pl.BlockSpec((1, s, HEAD_DIM), lambda bi, hi: (bi, 0, 0))
    return pl.pallas_call(
        functools.partial(_attn_body, out_scale=out_scale),
        out_shape=jax.ShapeDtypeStruct((b, s, width), BF16),
        grid=(b, heads),
        in_specs=[
            pl.BlockSpec(memory_space=pltpu.SMEM),
            sect(0), sect(1), sect(2), table, table, row, row,
            pl.BlockSpec((1, HEAD_DIM), lambda bi, hi: (0, hi)),
        ],
        out_specs=pl.BlockSpec((1, s, HEAD_DIM), lambda bi, hi: (bi, 0, hi)),
        scratch_shapes=[pltpu.VMEM((s, HEAD_DIM), BF16) for _ in range(4)],
        compiler_params=_params("parallel", "parallel"),
        name="attn",
    )(lam.reshape(1), p3, p3, p3, cos_t, sin_t, tiled_gain(q_gain), tiled_gain(k_gain),
      out_gain.reshape(1, width))


def _conv_body(b_ref, c_ref, u_ref, w_ref, bias_ref, gain_ref, o_ref):
    s = u_ref.shape[1]
    v = c_ref[0] * u_ref[0]
    row = lax.broadcasted_iota(jnp.int32, (s, 1), 0)
    prev = jnp.where(row == 0, 0.0, pltpu.roll(v, 1, 0))
    nxt = jnp.where(row == s - 1, 0.0, pltpu.roll(v, s - 1, 0))
    y = bias_ref[...] + prev * w_ref[0:1, :]
    y = y + v * w_ref[1:2, :]
    y = y + nxt * w_ref[2:3, :]
    y = b_ref[0] * y
    o_ref[0] = _rms_norm_rows(y, gain_ref[...]).astype(o_ref.dtype)


def _conv(p3, w, bias, gain, col0, width):
    b, s, _ = p3.shape
    groups = width // HEAD_DIM
    blk0 = col0 // HEAD_DIM

    def sect(k):
        return pl.BlockSpec((1, s, HEAD_DIM), lambda bi, gi, k=k: (bi, 0, blk0 + k * groups + gi))

    return pl.pallas_call(
        _conv_body,
        out_shape=jax.ShapeDtypeStruct((b, s, width), BF16),
        grid=(b, groups),
        in_specs=[
            sect(0), sect(1), sect(2),
            pl.BlockSpec((w.shape[0], HEAD_DIM), lambda bi, gi: (0, gi)),
            pl.BlockSpec((1, HEAD_DIM), lambda bi, gi: (0, gi)),
            pl.BlockSpec((1, HEAD_DIM), lambda bi, gi: (0, gi)),
        ],
        out_specs=pl.BlockSpec((1, s, HEAD_DIM), lambda bi, gi: (bi, 0, gi)),
        compiler_params=_params("parallel", "parallel"),
        name="conv",
    )(p3, p3, p3, w, bias.reshape(1, width), gain.reshape(1, width))


def _proj_out_body(x_ref, ya_ref, yb_ref, yc_ref, wa_ref, wb_ref, wc_ref, o_ref):
    acc = jnp.dot(ya_ref[...], wa_ref[...], preferred_element_type=F32)
    acc = acc + jnp.dot(yb_ref[...], wb_ref[...], preferred_element_type=F32)
    acc = acc + jnp.dot(yc_ref[...], wc_ref[...], preferred_element_type=F32)
    o_ref[...] = x_ref[...] + acc


def _proj_out(x2, ya, yb, yc, w):
    t, d = x2.shape
    wa, wb, wc = ya.shape[1], yb.shape[1], yc.shape[1]
    assert wa % wb == 0 and (wa + wb) % wc == 0, (wa, wb, wc)
    tm = _tile(t, 512, 8)

    def rows(width):
        return pl.BlockSpec((tm, width), lambda i: (i, 0))

    def w_rows(n_rows, row0):
        return pl.BlockSpec((n_rows, d), lambda i, blk=row0 // n_rows: (blk, 0))

    return pl.pallas_call(
        _proj_out_body,
        out_shape=jax.ShapeDtypeStruct((t, d), F32),
        grid=(t // tm,),
        in_specs=[rows(d), rows(wa), rows(wb), rows(wc),
                  w_rows(wa, 0), w_rows(wb, wa), w_rows(wc, wa + wb)],
        out_specs=rows(d),
        compiler_params=_params("parallel"),
        name="proj_out",
    )(x2, ya, yb, yc, w, w, w)


def _ffn_weights(wg, wu, wd):
    f = wg.shape[-1]
    pad = (-f) % FFN_TILE if f > FFN_TILE else 0
    wg = jnp.pad(wg.astype(BF16), ((0, 0), (0, 0), (0, pad)))
    wu = jnp.pad(wu.astype(BF16), ((0, 0), (0, 0), (0, pad)))
    wd = jnp.pad(wd.astype(BF16), ((0, 0), (0, pad), (0, 0)))
    return wg, wu, wd


def _attn_lane_order(w, col0, width):
    depth, d, _ = w.shape
    half = ATT_MAP_DIM // 2
    qk = w[:, :, col0:col0 + 2 * width].reshape(depth, d, 2 * width // HEAD_DIM, 2, 2, half)
    qk = qk.transpose(0, 1, 2, 4, 3, 5).reshape(depth, d, 2 * width)
    return lax.dynamic_update_slice(w, qk, (0, 0, col0))


def kernel(x, positions, ffn1_norm, ffn1_w_gate, ffn1_w_up, ffn1_w_down, mix_norm, w_in, hgrn_lb_logits, hgrn_norm, da_q_norm, da_k_norm, da_lambda_q1, da_lambda_k1, da_lambda_q2, da_lambda_k2, da_out_norm, conv_w, conv_b, conv_norm, w_out, ffn2_norm, ffn2_w_gate, ffn2_w_up, ffn2_w_down):
    b, s, d = x.shape
    depth = w_in.shape[0]
    hg_width = d // 2
    da_width = d // 4
    sc_width = d - hg_width - da_width
    da_col0 = 5 * hg_width
    sc_col0 = da_col0 + 3 * da_width

    half = ATT_MAP_DIM // 2
    inv_freq = ROPE_THETA ** (-jnp.arange(0, ATT_MAP_DIM, 2, dtype=F32) / ATT_MAP_DIM)
    ang = positions.astype(F32)[..., None] * inv_freq
    cos_t = jnp.tile(jnp.cos(ang), (1, 1, HEAD_DIM // half))
    sin_half = jnp.sin(ang)
    sin_t = jnp.concatenate([-sin_half, -sin_half, sin_half, sin_half], axis=-1)

    lb_all = jnp.cumsum(jax.nn.softmax(hgrn_lb_logits.astype(F32), axis=1), axis=1)
    lb_all = lb_all - lb_all[:, :1]

    ffn1_w = _ffn_weights(ffn1_w_gate, ffn1_w_up, ffn1_w_down)
    ffn2_w = _ffn_weights(ffn2_w_gate, ffn2_w_up, ffn2_w_down)
    w_in16 = _attn_lane_order(w_in.astype(BF16), da_col0, da_width)
    w_out16 = w_out.astype(BF16)

    x2 = x.reshape(b * s, d)
    for layer in range(depth):
        x2 = _ffn(x2, ffn1_norm[layer], *(w[layer] for w in ffn1_w))

        p3 = _proj_in(x2, mix_norm[layer], w_in16[layer]).reshape(b, s, -1)

        y_a = _hgrn(p3, lb_all[:, layer], hgrn_norm[layer], hg_width)

        lam_init = 0.8 - 0.6 * math.exp(-0.3 * layer)
        lam = (jnp.exp(jnp.sum(da_lambda_q1[layer].astype(F32) * da_lambda_k1[layer].astype(F32)))
               - jnp.exp(jnp.sum(da_lambda_q2[layer].astype(F32) * da_lambda_k2[layer].astype(F32)))
               + lam_init)
        y_b = _attn(p3, cos_t, sin_t, lam, da_q_norm[layer], da_k_norm[layer], da_out_norm[layer],
                    da_col0, da_width, 1.0 - lam_init)

        y_c = _conv(p3, conv_w[layer], conv_b[layer], conv_norm[layer], sc_col0, sc_width)

        x2 = _proj_out(x2, y_a.reshape(b * s, -1), y_b.reshape(b * s, -1), y_c.reshape(b * s, -1),
                       w_out16[layer])

        x2 = _ffn(x2, ffn2_norm[layer], *(w[layer] for w in ffn2_w))
    return x2.reshape(b, s, d)
```

```python
import functools
import math

import jax
import jax.numpy as jnp
import numpy as np
from jax import lax
from jax.experimental import pallas as pl
from jax.experimental.pallas import tpu as pltpu

F32 = jnp.float32
BF16 = jnp.bfloat16

EPS = 1e-6
ROPE_THETA = 10000.0
LANES = 128
HEAD_DIM = LANES
ATT_MAP_DIM = 64
HGRN_CHUNK = 128
ATT_Q_BLOCK = 256
FFN_TILE = 512
VMEM_LIMIT_BYTES = 56 * 1024 * 1024

_NT = (((1,), (1,)), ((), ()))


def _rms_norm_rows(x, gain):
    ms = jnp.mean(x * x, axis=-1, keepdims=True)
    return x * lax.rsqrt(ms + EPS) * gain


def _tile(n, pref, quantum):
    if n <= pref:
        return n
    t = (pref // quantum) * quantum
    while n % t:
        t -= quantum
    return t


def _params(*semantics):
    return pltpu.CompilerParams(dimension_semantics=semantics, vmem_limit_bytes=VMEM_LIMIT_BYTES)


def _ffn_body(x_ref, gain_ref, wg_ref, wu_ref, wd_ref, o_ref, h_ref):
    j = pl.program_id(1)

    @pl.when(j == 0)
    def _():
        x = x_ref[...]
        h_ref[...] = _rms_norm_rows(x, gain_ref[...]).astype(BF16)
        o_ref[...] = x

    h = h_ref[...]
    g = jnp.dot(h, wg_ref[...], preferred_element_type=F32)
    u = jnp.dot(h, wu_ref[...], preferred_element_type=F32)
    a = (g * jax.nn.sigmoid(g) * (0.5 * u)).astype(BF16)
    o_ref[...] += jnp.dot(a, wd_ref[...], preferred_element_type=F32)


def _ffn(x2, gain, wg, wu, wd):
    t, d = x2.shape
    fp = wg.shape[1]
    tm = _tile(t, 1024, 8)
    tf = _tile(fp, FFN_TILE, LANES)
    return pl.pallas_call(
        _ffn_body,
        out_shape=jax.ShapeDtypeStruct((t, d), F32),
        grid=(t // tm, fp // tf),
        in_specs=[
            pl.BlockSpec((tm, d), lambda i, j: (i, 0)),
            pl.BlockSpec((1, d), lambda i, j: (0, 0)),
            pl.BlockSpec((d, tf), lambda i, j: (0, j)),
            pl.BlockSpec((d, tf), lambda i, j: (0, j)),
            pl.BlockSpec((tf, d), lambda i, j: (j, 0)),
        ],
        out_specs=pl.BlockSpec((tm, d), lambda i, j: (i, 0)),
        scratch_shapes=[pltpu.VMEM((tm, d), BF16)],
        compiler_params=_params("parallel", "arbitrary"),
        name="ffn",
    )(x2, gain.reshape(1, d), wg, wu, wd)


def _proj_in_body(x_ref, gain_ref, w_ref, o_ref, h_ref):
    @pl.when(pl.program_id(1) == 0)
    def _():
        h_ref[...] = _rms_norm_rows(x_ref[...], gain_ref[...]).astype(BF16)

    o_ref[...] = jnp.dot(h_ref[...], w_ref[...], preferred_element_type=F32)


def _proj_in(x2, gain, w):
    t, d = x2.shape
    n = w.shape[1]
    tm = _tile(t, 1024, 8)
    tn = _tile(n, 1024, LANES)
    return pl.pallas_call(
        _proj_in_body,
        out_shape=jax.ShapeDtypeStruct((t, n), F32),
        grid=(t // tm, n // tn),
        in_specs=[
            pl.BlockSpec((tm, d), lambda i, j: (i, 0)),
            pl.BlockSpec((1, d), lambda i, j: (0, 0)),
            pl.BlockSpec((d, tn), lambda i, j: (0, j)),
        ],
        out_specs=pl.BlockSpec((tm, tn), lambda i, j: (i, j)),
        scratch_shapes=[pltpu.VMEM((tm, d), BF16)],
        compiler_params=_params("parallel", "arbitrary"),
        name="proj_in",
    )(x2, gain.reshape(1, d), w)


def _hgrn_constants(c):
    t = np.arange(c)[:, None]
    r = np.arange(c)[None, :]
    masks, small = [], []
    h = c // 2
    while h >= 1:
        start = (t // h) * h
        upper = (t // h) % 2 == 1
        masks.append((t // (2 * h) == r // (2 * h)) & upper & ((r // h) % 2 == 0))
        if h <= 2:
            small.append(np.where(upper, (r >= start) & (r <= t), (r > t) & (r <= start + h - 1)))
        h //= 2
    masks = np.stack(masks).astype(np.float32)
    sums = np.stack([r <= t] + small).astype(np.float32)
    out = []
    for flip in (False, True):
        m = masks[:, ::-1, ::-1] if flip else masks
        s = (sums[:, ::-1, ::-1] if flip else sums).reshape(-1, c)
        out.append(jnp.asarray(np.concatenate([s, s], axis=1), dtype=BF16))
        out.append(jnp.asarray(m, dtype=F32))
    return out


def _neg_abs(d):
    sign = jnp.int32(-2 ** 31)
    return lax.bitcast_convert_type(lax.bitcast_convert_type(d, jnp.int32) | sign, F32)


def _hgrn_level_exponents(cum, cum_ref, reverse):
    c = cum.shape[0]
    xs = []
    h = c // 2
    while h >= 4:
        pieces = []
        for j in range(c // (2 * h)):
            r0 = j * 2 * h
            b = cum_ref[pl.ds(r0 + h if reverse else r0 + h - 1, 1), :]
            if h >= 8:
                lo_rows, hi_rows = cum[r0:r0 + h], cum[r0 + h:r0 + 2 * h]
                pieces.extend((lo_rows - b, b - hi_rows) if reverse else (b - lo_rows, hi_rows - b))
            else:
                pieces.append(_neg_abs(cum[r0:r0 + 2 * h] - b))
        xs.append(jnp.concatenate(pieces, axis=0))
        h //= 2
    return xs


def _hgrn_chunk(q, v, z, lb, sums_ref, masks_ref, state_ref, cum_ref, reverse):
    c = q.shape[0]
    f = lb + (1.0 - lb) * jax.nn.sigmoid(z)
    k = 1.0 - f
    lf = jnp.log2(f)
    hi = lf.astype(BF16)
    lo = (lf - hi.astype(F32)).astype(BF16)
    sums = jnp.dot(sums_ref[...], jnp.concatenate([hi, lo], axis=0), preferred_element_type=F32)
    cum = sums[:c]
    cum_ref[...] = cum
    total_row = 0 if reverse else c - 1
    x_key = cum_ref[pl.ds(total_row, 1), :] - cum
    x_levels = _hgrn_level_exponents(cum, cum_ref, reverse) + [sums[c:2 * c], sums[2 * c:]]
    e_levels = [jnp.exp2(x).astype(BF16) for x in x_levels]
    e_q_all = jnp.exp2(cum)
    e_k_all = jnp.exp2(x_key).astype(BF16)
    q16, k16, v16 = q.astype(BF16), k.astype(BF16), v.astype(BF16)
    outs = []
    for a in range(q.shape[1] // HEAD_DIM):
        sl = slice(a * HEAD_DIM, (a + 1) * HEAD_DIM)
        qa, ka, va = q16[:, sl], k16[:, sl], v16[:, sl]
        ka_t = ka.T
        scores = jnp.zeros((c, c), F32)
        for lvl, e_l in enumerate(e_levels):
            el = e_l[:, sl]
            s_l = jnp.dot(qa * el, ka_t * el.T, preferred_element_type=F32)
            scores = scores + s_l * masks_ref[lvl]
        e_q = e_q_all[:, sl]
        state_t = state_ref[a]
        o = jnp.dot(scores.astype(BF16), va, preferred_element_type=F32)
        o = o + jnp.sum(q[:, sl] * k[:, sl], axis=-1, keepdims=True) * v[:, sl]
        o = o + lax.dot_general(qa * e_q.astype(BF16), state_t.astype(BF16), _NT,
                                preferred_element_type=F32)
        state_ref[a] = state_t * e_q[total_row:total_row + 1, :] + jnp.dot(
            v[:, sl].T.astype(BF16), ka * e_k_all[:, sl], preferred_element_type=F32)
        outs.append(o)
    return jnp.concatenate(outs, axis=-1)


def _hgrn_body(q_ref, v_ref, zf_ref, zb_ref, g_ref, lb_ref, gain_ref,
               sums_f_ref, masks_f_ref, sums_b_ref, masks_b_ref,
               o_ref, of_ref, ob_ref, state_f_ref, state_b_ref, cum_ref):
    c = masks_f_ref.shape[1]
    n_chunks = q_ref.shape[1] // c
    per_step = cum_ref.shape[0] // 2
    state_f_ref[...] = jnp.zeros_like(state_f_ref)
    state_b_ref[...] = jnp.zeros_like(state_b_ref)

    def scan_step(n, carry):
        for u in range(per_step):
            m = n * per_step + u
            rf = pl.multiple_of(m * c, c)
            rb = pl.multiple_of((n_chunks - 1 - m) * c, c)
            of_ref[pl.ds(rf, c), :] = _hgrn_chunk(
                q_ref[0, pl.ds(rf, c), :], v_ref[0, pl.ds(rf, c), :], zf_ref[0, pl.ds(rf, c), :],
                lb_ref[0:1, :], sums_f_ref, masks_f_ref, state_f_ref, cum_ref.at[2 * u], False)
            ob_ref[pl.ds(rb, c), :] = _hgrn_chunk(
                q_ref[0, pl.ds(rb, c), :], v_ref[0, pl.ds(rb, c), :], zb_ref[0, pl.ds(rb, c), :],
                lb_ref[1:2, :], sums_b_ref, masks_b_ref, state_b_ref, cum_ref.at[2 * u + 1], True)
        return carry

    lax.fori_loop(0, n_chunks // per_step, scan_step, 0)

    def finish_step(n, carry):
        r = pl.multiple_of(n * c, c)
        o = of_ref[pl.ds(r, c), :] + ob_ref[pl.ds(r, c), :]
        g = g_ref[0, pl.ds(r, c), :]
        normed = [
            _rms_norm_rows(o[:, a * HEAD_DIM:(a + 1) * HEAD_DIM],
                           gain_ref[:, a * HEAD_DIM:(a + 1) * HEAD_DIM])
            for a in range(o.shape[1] // HEAD_DIM)
        ]
        y = jnp.concatenate(normed, axis=-1) * (g * jax.nn.sigmoid(g))
        o_ref[0, pl.ds(r, c), :] = y.astype(o_ref.dtype)
        return carry

    lax.fori_loop(0, n_chunks, finish_step, 0)


def _hgrn(p3, lb, gain, width):
    b, s, _ = p3.shape
    heads = width // HEAD_DIM
    nh = 2 if heads % 2 == 0 else 1
    wb = nh * HEAD_DIM
    n_blk = width // wb
    c = min(HGRN_CHUNK, s)
    sums_f, masks_f, sums_b, masks_b = _hgrn_constants(c)
    per_step = 4 if (s // c) % 4 == 0 else 1

    def sect(k):
        return pl.BlockSpec((1, s, wb), lambda bi, hi, k=k: (bi, 0, k * n_blk + hi))

    def whole(arr):
        return pl.BlockSpec(arr.shape, lambda bi, hi, nd=arr.ndim: (0,) * nd)

    return pl.pallas_call(
        _hgrn_body,
        out_shape=jax.ShapeDtypeStruct((b, s, width), BF16),
        grid=(b, n_blk),
        in_specs=[
            sect(0), sect(1), sect(2), sect(3), sect(4),
            pl.BlockSpec((2, wb), lambda bi, hi: (0, hi)),
            pl.BlockSpec((1, wb), lambda bi, hi: (0, hi)),
            whole(sums_f), whole(masks_f), whole(sums_b), whole(masks_b),
        ],
        out_specs=pl.BlockSpec((1, s, wb), lambda bi, hi: (bi, 0, hi)),
        scratch_shapes=[
            pltpu.VMEM((s, wb), F32), pltpu.VMEM((s, wb), F32),
            pltpu.VMEM((nh, HEAD_DIM, HEAD_DIM), F32), pltpu.VMEM((nh, HEAD_DIM, HEAD_DIM), F32),
            pltpu.VMEM((2 * per_step, c, wb), F32),
        ],
        compiler_params=_params("parallel", "parallel"),
        name="hgrn",
    )(p3, p3, p3, p3, p3, lb, gain.reshape(1, width), sums_f, masks_f, sums_b, masks_b)


def _attn_body(lam_ref, q_ref, k_ref, v_ref, cos_ref, sin_ref, qg_ref, kg_ref, og_ref,
               o_ref, q1_ref, q2_ref, kb_ref, vb_ref, sc_ref, *, out_scale):
    s = q_ref.shape[1]
    quarter = ATT_MAP_DIM // 2

    def lane_map(shape, axis):
        return (lax.broadcasted_iota(jnp.int32, shape, axis) // quarter) % 2

    first_map = lane_map((1, HEAD_DIM), 1) == 0
    square = (HEAD_DIM, HEAD_DIM)
    same_map = jnp.where(lane_map(square, 0) == lane_map(square, 1), 1.0, 0.0).astype(BF16)
    cos = cos_ref[0]
    sin = sin_ref[0]

    def prep(t, gain):
        sq = t * t
        hi = sq.astype(BF16)
        lo = (sq - hi.astype(F32)).astype(BF16)
        ms = (jnp.dot(hi, same_map, preferred_element_type=F32)
              + jnp.dot(lo, same_map, preferred_element_type=F32)) * (1.0 / ATT_MAP_DIM)
        y = t * lax.rsqrt(ms + EPS) * gain
        return y * cos + pltpu.roll(y, HEAD_DIM // 2, 1) * sin

    qr = prep(q_ref[0], qg_ref[...]) * (ATT_MAP_DIM ** -0.5 * math.log2(math.e))
    q1_ref[...] = jnp.where(first_map, qr, 0.0).astype(BF16)
    q2_ref[...] = jnp.where(first_map, 0.0, qr).astype(BF16)
    kb_ref[...] = prep(k_ref[0], kg_ref[...]).astype(BF16)
    vb_ref[...] = v_ref[0].astype(BF16)
    lam = lam_ref[0]
    tq = min(ATT_Q_BLOCK, s)

    n_blocks = s // tq

    def scores(n, slot):
        r = pl.multiple_of(n * tq, tq)
        for m, qm_ref in enumerate((q1_ref, q2_ref)):
            sc_ref[slot, m] = lax.dot_general(qm_ref[pl.ds(r, tq), :], kb_ref[...], _NT,
                                              preferred_element_type=F32)

    def finish(n, slot):
        r = pl.multiple_of(n * tq, tq)
        e, inv = [], []
        for m in range(2):
            sc = sc_ref[slot, m]
            em = jnp.exp2(sc - jnp.max(sc, axis=-1, keepdims=True))
            e.append(em.astype(BF16))
            inv.append(1.0 / jnp.sum(em, axis=-1, keepdims=True))
        a = e[0] * inv[0].astype(BF16) - e[1] * (lam * inv[1]).astype(BF16)
        o = jnp.dot(a, vb_ref[...], preferred_element_type=F32)
        y = _rms_norm_rows(o, og_ref[...]) * out_scale
        o_ref[0, pl.ds(r, tq), :] = y.astype(o_ref.dtype)

    def scores_ahead(n, slot):
        scores(n, slot)
        tail = sc_ref[slot, 1, tq - 16:, s - HEAD_DIM:]
        zero = (pltpu.bitcast(tail, jnp.uint32) >> 16) >> 16
        vb_ref[0:16, :] = vb_ref[0:16, :] + zero.astype(F32).astype(BF16)

    assert n_blocks == 1 or n_blocks % 2 == 0, n_blocks
    scores(0, 0)
    if n_blocks > 1:
        def pair(m, carry):
            scores_ahead(2 * m + 1, 1)
            finish(2 * m, 0)
            scores_ahead(2 * m + 2, 0)
            finish(2 * m + 1, 1)
            return carry

        lax.fori_loop(0, n_blocks // 2 - 1, pair, 0)
        scores_ahead(n_blocks - 1, 1)
        finish(n_blocks - 2, 0)
        finish(n_blocks - 1, 1)
    else:
        finish(0, 0)


def _attn(p3, cos_t, sin_t, lam, q_gain, k_gain, out_gain, col0, width, out_scale):
    b, s, _ = p3.shape
    heads = width // HEAD_DIM
    blk0 = col0 // HEAD_DIM

    def sect(k):
        return pl.BlockSpec((1, s, HEAD_DIM), lambda bi, hi, k=k: (bi, 0, blk0 + k * heads + hi))

    def tiled_gain(g):
        half = ATT_MAP_DIM // 2
        return jnp.concatenate([g[:half], g[:half], g[half:], g[half:]]).reshape(1, HEAD_DIM)

    row = pl.BlockSpec((1, HEAD_DIM), lambda bi, hi: (0, 0))
    table = pl.BlockSpec((1, s, HEAD_DIM), lambda bi, hi: (bi, 0, 0))
    return pl.pallas_call(
        functools.partial(_attn_body, out_scale=out_scale),
        out_shape=jax.ShapeDtypeStruct((b, s, width), BF16),
        grid=(b, heads),
        in_specs=[
            pl.BlockSpec(memory_space=pltpu.SMEM),
            sect(0), sect(1), sect(2), table, table, row, row,
            pl.BlockSpec((1, HEAD_DIM), lambda bi, hi: (0, hi)),
        ],
        out_specs=pl.BlockSpec((1, s, HEAD_DIM), lambda bi, hi: (bi, 0, hi)),
        scratch_shapes=[pltpu.VMEM((s, HEAD_DIM), BF16) for _ in range(4)]
        + [pltpu.VMEM((2, 2, min(ATT_Q_BLOCK, s), s), F32)],
        compiler_params=_params("parallel", "parallel"),
        name="attn",
    )(lam.reshape(1), p3, p3, p3, cos_t, sin_t, tiled_gain(q_gain), tiled_gain(k_gain),
      out_gain.reshape(1, width))


def _conv_body(b_ref, c_ref, u_ref, w_ref, bias_ref, gain_ref, o_ref):
    s = u_ref.shape[1]
    v = c_ref[0] * u_ref[0]
    row = lax.broadcasted_iota(jnp.int32, (s, 1), 0)
    prev = jnp.where(row == 0, 0.0, pltpu.roll(v, 1, 0))
    nxt = jnp.where(row == s - 1, 0.0, pltpu.roll(v, s - 1, 0))
    y = bias_ref[...] + prev * w_ref[0:1, :]
    y = y + v * w_ref[1:2, :]
    y = y + nxt * w_ref[2:3, :]
    y = b_ref[0] * y
    o_ref[0] = _rms_norm_rows(y, gain_ref[...]).astype(o_ref.dtype)


def _conv(p3, w, bias, gain, col0, width):
    b, s, _ = p3.shape
    groups = width // HEAD_DIM
    blk0 = col0 // HEAD_DIM

    def sect(k):
        return pl.BlockSpec((1, s, HEAD_DIM), lambda bi, gi, k=k: (bi, 0, blk0 + k * groups + gi))

    return pl.pallas_call(
        _conv_body,
        out_shape=jax.ShapeDtypeStruct((b, s, width), BF16),
        grid=(b, groups),
        in_specs=[
            sect(0), sect(1), sect(2),
            pl.BlockSpec((w.shape[0], HEAD_DIM), lambda bi, gi: (0, gi)),
            pl.BlockSpec((1, HEAD_DIM), lambda bi, gi: (0, gi)),
            pl.BlockSpec((1, HEAD_DIM), lambda bi, gi: (0, gi)),
        ],
        out_specs=pl.BlockSpec((1, s, HEAD_DIM), lambda bi, gi: (bi, 0, gi)),
        compiler_params=_params("parallel", "parallel"),
        name="conv",
    )(p3, p3, p3, w, bias.reshape(1, width), gain.reshape(1, width))


def _proj_out_body(x_ref, ya_ref, yb_ref, yc_ref, wa_ref, wb_ref, wc_ref, o_ref):
    acc = jnp.dot(ya_ref[...], wa_ref[...], preferred_element_type=F32)
    acc = acc + jnp.dot(yb_ref[...], wb_ref[...], preferred_element_type=F32)
    acc = acc + jnp.dot(yc_ref[...], wc_ref[...], preferred_element_type=F32)
    o_ref[...] = x_ref[...] + acc


def _proj_out(x2, ya, yb, yc, w):
    t, d = x2.shape
    wa, wb, wc = ya.shape[1], yb.shape[1], yc.shape[1]
    assert wa % wb == 0 and (wa + wb) % wc == 0, (wa, wb, wc)
    tm = _tile(t, 512, 8)

    def rows(width):
        return pl.BlockSpec((tm, width), lambda i: (i, 0))

    def w_rows(n_rows, row0):
        return pl.BlockSpec((n_rows, d), lambda i, blk=row0 // n_rows: (blk, 0))

    return pl.pallas_call(
        _proj_out_body,
        out_shape=jax.ShapeDtypeStruct((t, d), F32),
        grid=(t // tm,),
        in_specs=[rows(d), rows(wa), rows(wb), rows(wc),
                  w_rows(wa, 0), w_rows(wb, wa), w_rows(wc, wa + wb)],
        out_specs=rows(d),
        compiler_params=_params("parallel"),
        name="proj_out",
    )(x2, ya, yb, yc, w, w, w)


def _ffn_weights(wg, wu, wd):
    f = wg.shape[-1]
    pad = (-f) % FFN_TILE if f > FFN_TILE else 0

    def cast_into_zeros(w, shape):
        return lax.dynamic_update_slice(jnp.zeros(shape, BF16), w.astype(BF16), (0, 0, 0))

    depth, d, _ = wg.shape
    return (cast_into_zeros(wg, (depth, d, f + pad)), cast_into_zeros(wu, (depth, d, f + pad)),
            cast_into_zeros(wd, (depth, f + pad, d)))


def _attn_lane_order(w, col0, width):
    depth, d, _ = w.shape
    half = ATT_MAP_DIM // 2
    qk = w[:, :, col0:col0 + 2 * width].reshape(depth, d, 2 * width // HEAD_DIM, 2, 2, half)
    qk = qk.transpose(0, 1, 2, 4, 3, 5).reshape(depth, d, 2 * width)
    return lax.dynamic_update_slice(w, qk, (0, 0, col0))


def kernel(x, positions, ffn1_norm, ffn1_w_gate, ffn1_w_up, ffn1_w_down, mix_norm, w_in, hgrn_lb_logits, hgrn_norm, da_q_norm, da_k_norm, da_lambda_q1, da_lambda_k1, da_lambda_q2, da_lambda_k2, da_out_norm, conv_w, conv_b, conv_norm, w_out, ffn2_norm, ffn2_w_gate, ffn2_w_up, ffn2_w_down):
    b, s, d = x.shape
    depth = w_in.shape[0]
    hg_width = d // 2
    da_width = d // 4
    sc_width = d - hg_width - da_width
    da_col0 = 5 * hg_width
    sc_col0 = da_col0 + 3 * da_width

    half = ATT_MAP_DIM // 2
    inv_freq = ROPE_THETA ** (-jnp.arange(0, ATT_MAP_DIM, 2, dtype=F32) / ATT_MAP_DIM)
    ang = positions.astype(F32)[..., None] * inv_freq
    cos_t = jnp.tile(jnp.cos(ang), (1, 1, HEAD_DIM // half))
    sin_half = jnp.sin(ang)
    sin_t = jnp.concatenate([-sin_half, -sin_half, sin_half, sin_half], axis=-1)

    lb_all = jnp.cumsum(jax.nn.softmax(hgrn_lb_logits.astype(F32), axis=1), axis=1)
    lb_all = lb_all - lb_all[:, :1]

    ffn1_w = _ffn_weights(ffn1_w_gate, ffn1_w_up, ffn1_w_down)
    ffn2_w = _ffn_weights(ffn2_w_gate, ffn2_w_up, ffn2_w_down)
    w_in16 = _attn_lane_order(w_in.astype(BF16), da_col0, da_width)
    w_out16 = w_out.astype(BF16)

    x2 = x.reshape(b * s, d)
    for layer in range(depth):
        x2 = _ffn(x2, ffn1_norm[layer], *(w[layer] for w in ffn1_w))

        p3 = _proj_in(x2, mix_norm[layer], w_in16[layer]).reshape(b, s, -1)

        y_a = _hgrn(p3, lb_all[:, layer], hgrn_norm[layer], hg_width)

        lam_init = 0.8 - 0.6 * math.exp(-0.3 * layer)
        lam = (jnp.exp(jnp.sum(da_lambda_q1[layer].astype(F32) * da_lambda_k1[layer].astype(F32)))
               - jnp.exp(jnp.sum(da_lambda_q2[layer].astype(F32) * da_lambda_k2[layer].astype(F32)))
               + lam_init)
        y_b = _attn(p3, cos_t, sin_t, lam, da_q_norm[layer], da_k_norm[layer], da_out_norm[layer],
                    da_col0, da_width, 1.0 - lam_init)

        y_c = _conv(p3, conv_w[layer], conv_b[layer], conv_norm[layer], sc_col0, sc_width)

        x2 = _proj_out(x2, y_a.reshape(b * s, -1), y_b.reshape(b * s, -1), y_c.reshape(b * s, -1),
                       w_out16[layer])

        x2 = _ffn(x2, ffn2_norm[layer], *(w[layer] for w in ffn2_w))
    return x2.reshape(b, s, d)
```

```python
import functools
import math

import jax
import jax.numpy as jnp
import numpy as np
from jax import lax
from jax.experimental import pallas as pl
from jax.experimental.pallas import tpu as pltpu

F32 = jnp.float32
BF16 = jnp.bfloat16

EPS = 1e-6
ROPE_THETA = 10000.0
LANES = 128
HEAD_DIM = LANES
ATT_MAP_DIM = 64
HGRN_CHUNK = 128
ATT_Q_BLOCK = 256
FFN_TILE = 512
VMEM_LIMIT_BYTES = 56 * 1024 * 1024

_NT = (((1,), (1,)), ((), ()))


def _rms_norm_rows(x, gain):
    ms = jnp.mean(x * x, axis=-1, keepdims=True)
    return x * lax.rsqrt(ms + EPS) * gain


def _tile(n, pref, quantum):
    if n <= pref:
        return n
    t = (pref // quantum) * quantum
    while n % t:
        t -= quantum
    return t


def _params(*semantics):
    return pltpu.CompilerParams(dimension_semantics=semantics, vmem_limit_bytes=VMEM_LIMIT_BYTES)


def _ffn_body(x_ref, gain_ref, wg_ref, wu_ref, wd_ref, o_ref, h_ref):
    j = pl.program_id(1)

    @pl.when(j == 0)
    def _():
        x = x_ref[...]
        h_ref[...] = _rms_norm_rows(x, gain_ref[...]).astype(BF16)
        o_ref[...] = x

    h = h_ref[...]
    g = jnp.dot(h, wg_ref[...], preferred_element_type=F32)
    u = jnp.dot(h, wu_ref[...], preferred_element_type=F32)
    a = (g * jax.nn.sigmoid(g) * (0.5 * u)).astype(BF16)
    o_ref[...] += jnp.dot(a, wd_ref[...], preferred_element_type=F32)


def _ffn(x2, gain, wg, wu, wd):
    t, d = x2.shape
    fp = wg.shape[1]
    tm = _tile(t, 1024, 8)
    tf = _tile(fp, FFN_TILE, LANES)
    return pl.pallas_call(
        _ffn_body,
        out_shape=jax.ShapeDtypeStruct((t, d), F32),
        grid=(t // tm, fp // tf),
        in_specs=[
            pl.BlockSpec((tm, d), lambda i, j: (i, 0)),
            pl.BlockSpec((1, d), lambda i, j: (0, 0)),
            pl.BlockSpec((d, tf), lambda i, j: (0, j)),
            pl.BlockSpec((d, tf), lambda i, j: (0, j)),
            pl.BlockSpec((tf, d), lambda i, j: (j, 0)),
        ],
        out_specs=pl.BlockSpec((tm, d), lambda i, j: (i, 0)),
        scratch_shapes=[pltpu.VMEM((tm, d), BF16)],
        compiler_params=_params("parallel", "arbitrary"),
        name="ffn",
    )(x2, gain.reshape(1, d), wg, wu, wd)


def _proj_in_body(x_ref, gain_ref, w_ref, o_ref, h_ref):
    @pl.when(pl.program_id(1) == 0)
    def _():
        h_ref[...] = _rms_norm_rows(x_ref[...], gain_ref[...]).astype(BF16)

    o_ref[...] = jnp.dot(h_ref[...], w_ref[...], preferred_element_type=F32)


def _proj_in(x2, gain, w):
    t, d = x2.shape
    n = w.shape[1]
    tm = _tile(t, 1024, 8)
    tn = _tile(n, 2048, LANES)
    return pl.pallas_call(
        _proj_in_body,
        out_shape=jax.ShapeDtypeStruct((t, n), F32),
        grid=(t // tm, n // tn),
        in_specs=[
            pl.BlockSpec((tm, d), lambda i, j: (i, 0)),
            pl.BlockSpec((1, d), lambda i, j: (0, 0)),
            pl.BlockSpec((d, tn), lambda i, j: (0, j)),
        ],
        out_specs=pl.BlockSpec((tm, tn), lambda i, j: (i, j)),
        scratch_shapes=[pltpu.VMEM((tm, d), BF16)],
        compiler_params=_params("parallel", "arbitrary"),
        name="proj_in",
    )(x2, gain.reshape(1, d), w)


def _hgrn_constants(c):
    t = np.arange(c)[:, None]
    r = np.arange(c)[None, :]
    masks, small = [], []
    h = c // 2
    while h >= 1:
        start = (t // h) * h
        upper = (t // h) % 2 == 1
        masks.append((t // (2 * h) == r // (2 * h)) & upper & ((r // h) % 2 == 0))
        if h <= 2:
            small.append(np.where(upper, (r >= start) & (r <= t), (r > t) & (r <= start + h - 1)))
        h //= 2
    masks = np.stack(masks).astype(np.float32)
    sums = np.stack([r <= t] + small).astype(np.float32)
    out = []
    for flip in (False, True):
        m = masks[:, ::-1, ::-1] if flip else masks
        s = (sums[:, ::-1, ::-1] if flip else sums).reshape(-1, c)
        out.append(jnp.asarray(np.concatenate([s, s], axis=1), dtype=BF16))
        out.append(jnp.asarray(m, dtype=F32))
    return out


def _neg_abs(d):
    sign = jnp.int32(-2 ** 31)
    return lax.bitcast_convert_type(lax.bitcast_convert_type(d, jnp.int32) | sign, F32)


def _hgrn_level_exponents(cum, cum_ref, reverse):
    c = cum.shape[0]
    xs = []
    h = c // 2
    while h >= 4:
        pieces = []
        for j in range(c // (2 * h)):
            r0 = j * 2 * h
            b = cum_ref[pl.ds(r0 + h if reverse else r0 + h - 1, 1), :]
            if h >= 8:
                lo_rows, hi_rows = cum[r0:r0 + h], cum[r0 + h:r0 + 2 * h]
                pieces.extend((lo_rows - b, b - hi_rows) if reverse else (b - lo_rows, hi_rows - b))
            else:
                pieces.append(_neg_abs(cum[r0:r0 + 2 * h] - b))
        xs.append(jnp.concatenate(pieces, axis=0))
        h //= 2
    return xs


def _hgrn_chunk(q, v, z, lb, sums_ref, masks_ref, state_ref, cum_ref, reverse):
    c = q.shape[0]
    f = lb + (1.0 - lb) * jax.nn.sigmoid(z)
    k = 1.0 - f
    lf = jnp.log2(f)
    hi = lf.astype(BF16)
    lo = (lf - hi.astype(F32)).astype(BF16)
    sums = jnp.dot(sums_ref[...], jnp.concatenate([hi, lo], axis=0), preferred_element_type=F32)
    cum = sums[:c]
    cum_ref[...] = cum
    total_row = 0 if reverse else c - 1
    x_key = cum_ref[pl.ds(total_row, 1), :] - cum
    x_levels = _hgrn_level_exponents(cum, cum_ref, reverse) + [sums[c:2 * c], sums[2 * c:]]
    e_levels = [jnp.exp2(x).astype(BF16) for x in x_levels]
    e_q_all = jnp.exp2(cum)
    e_k_all = jnp.exp2(x_key).astype(BF16)
    q16, k16, v16 = q.astype(BF16), k.astype(BF16), v.astype(BF16)
    outs = []
    for a in range(q.shape[1] // HEAD_DIM):
        sl = slice(a * HEAD_DIM, (a + 1) * HEAD_DIM)
        qa, ka, va = q16[:, sl], k16[:, sl], v16[:, sl]
        ka_t = ka.T
        scores = jnp.zeros((c, c), F32)
        for lvl, e_l in enumerate(e_levels):
            el = e_l[:, sl]
            s_l = jnp.dot(qa * el, ka_t * el.T, preferred_element_type=F32)
            scores = scores + s_l * masks_ref[lvl]
        e_q = e_q_all[:, sl]
        state_t = state_ref[a]
        o = jnp.dot(scores.astype(BF16), va, preferred_element_type=F32)
        o = o + jnp.sum(q[:, sl] * k[:, sl], axis=-1, keepdims=True) * v[:, sl]
        o = o + lax.dot_general(qa * e_q.astype(BF16), state_t.astype(BF16), _NT,
                                preferred_element_type=F32)
        state_ref[a] = state_t * e_q[total_row:total_row + 1, :] + jnp.dot(
            v[:, sl].T.astype(BF16), ka * e_k_all[:, sl], preferred_element_type=F32)
        outs.append(o)
    return jnp.concatenate(outs, axis=-1)


def _hgrn_body(q_ref, v_ref, zf_ref, zb_ref, g_ref, lb_ref, gain_ref,
               sums_f_ref, masks_f_ref, sums_b_ref, masks_b_ref,
               o_ref, of_ref, ob_ref, state_f_ref, state_b_ref, cum_ref):
    c = masks_f_ref.shape[1]
    n_chunks = q_ref.shape[1] // c
    per_step = cum_ref.shape[0] // 2
    state_f_ref[...] = jnp.zeros_like(state_f_ref)
    state_b_ref[...] = jnp.zeros_like(state_b_ref)

    def scan_step(n, carry):
        for u in range(per_step):
            m = n * per_step + u
            rf = pl.multiple_of(m * c, c)
            rb = pl.multiple_of((n_chunks - 1 - m) * c, c)
            of_ref[pl.ds(rf, c), :] = _hgrn_chunk(
                q_ref[0, pl.ds(rf, c), :], v_ref[0, pl.ds(rf, c), :], zf_ref[0, pl.ds(rf, c), :],
                lb_ref[0:1, :], sums_f_ref, masks_f_ref, state_f_ref, cum_ref.at[2 * u], False)
            ob_ref[pl.ds(rb, c), :] = _hgrn_chunk(
                q_ref[0, pl.ds(rb, c), :], v_ref[0, pl.ds(rb, c), :], zb_ref[0, pl.ds(rb, c), :],
                lb_ref[1:2, :], sums_b_ref, masks_b_ref, state_b_ref, cum_ref.at[2 * u + 1], True)
        return carry

    lax.fori_loop(0, n_chunks // per_step, scan_step, 0)

    def finish_step(n, carry):
        r = pl.multiple_of(n * c, c)
        o = of_ref[pl.ds(r, c), :] + ob_ref[pl.ds(r, c), :]
        g = g_ref[0, pl.ds(r, c), :]
        normed = [
            _rms_norm_rows(o[:, a * HEAD_DIM:(a + 1) * HEAD_DIM],
                           gain_ref[:, a * HEAD_DIM:(a + 1) * HEAD_DIM])
            for a in range(o.shape[1] // HEAD_DIM)
        ]
        y = jnp.concatenate(normed, axis=-1) * (g * jax.nn.sigmoid(g))
        o_ref[0, pl.ds(r, c), :] = y.astype(o_ref.dtype)
        return carry

    lax.fori_loop(0, n_chunks, finish_step, 0)


def _hgrn(p3, lb, gain, width):
    b, s, _ = p3.shape
    heads = width // HEAD_DIM
    nh = 2 if heads % 2 == 0 else 1
    wb = nh * HEAD_DIM
    n_blk = width // wb
    c = min(HGRN_CHUNK, s)
    sums_f, masks_f, sums_b, masks_b = _hgrn_constants(c)
    per_step = 4 if (s // c) % 4 == 0 else 1

    def sect(k):
        return pl.BlockSpec((1, s, wb), lambda bi, hi, k=k: (bi, 0, k * n_blk + hi))

    def whole(arr):
        return pl.BlockSpec(arr.shape, lambda bi, hi, nd=arr.ndim: (0,) * nd)

    return pl.pallas_call(
        _hgrn_body,
        out_shape=jax.ShapeDtypeStruct((b, s, width), BF16),
        grid=(b, n_blk),
        in_specs=[
            sect(0), sect(1), sect(2), sect(3), sect(4),
            pl.BlockSpec((2, wb), lambda bi, hi: (0, hi)),
            pl.BlockSpec((1, wb), lambda bi, hi: (0, hi)),
            whole(sums_f), whole(masks_f), whole(sums_b), whole(masks_b),
        ],
        out_specs=pl.BlockSpec((1, s, wb), lambda bi, hi: (bi, 0, hi)),
        scratch_shapes=[
            pltpu.VMEM((s, wb), F32), pltpu.VMEM((s, wb), F32),
            pltpu.VMEM((nh, HEAD_DIM, HEAD_DIM), F32), pltpu.VMEM((nh, HEAD_DIM, HEAD_DIM), F32),
            pltpu.VMEM((2 * per_step, c, wb), F32),
        ],
        compiler_params=_params("parallel", "parallel"),
        name="hgrn",
    )(p3, p3, p3, p3, p3, lb, gain.reshape(1, width), sums_f, masks_f, sums_b, masks_b)


def _attn_body(lam_ref, q_ref, k_ref, v_ref, cos_ref, sin_ref, qg_ref, kg_ref, og_ref,
               o_ref, q1_ref, q2_ref, kb_ref, vb_ref, sc_ref, *, out_scale):
    s = q_ref.shape[1]
    quarter = ATT_MAP_DIM // 2

    def lane_map(shape, axis):
        return (lax.broadcasted_iota(jnp.int32, shape, axis) // quarter) % 2

    first_map = lane_map((1, HEAD_DIM), 1) == 0
    square = (HEAD_DIM, HEAD_DIM)
    same_map = jnp.where(lane_map(square, 0) == lane_map(square, 1), 1.0, 0.0).astype(BF16)
    cos = cos_ref[0]
    sin = sin_ref[0]

    def prep(t, gain):
        sq = t * t
        hi = sq.astype(BF16)
        lo = (sq - hi.astype(F32)).astype(BF16)
        ms = (jnp.dot(hi, same_map, preferred_element_type=F32)
              + jnp.dot(lo, same_map, preferred_element_type=F32)) * (1.0 / ATT_MAP_DIM)
        y = t * lax.rsqrt(ms + EPS) * gain
        return y * cos + pltpu.roll(y, HEAD_DIM // 2, 1) * sin

    qr = prep(q_ref[0], qg_ref[...]) * (ATT_MAP_DIM ** -0.5 * math.log2(math.e))
    q1_ref[...] = jnp.where(first_map, qr, 0.0).astype(BF16)
    q2_ref[...] = jnp.where(first_map, 0.0, qr).astype(BF16)
    kb_ref[...] = prep(k_ref[0], kg_ref[...]).astype(BF16)
    vb_ref[...] = v_ref[0].astype(BF16)
    lam = lam_ref[0]
    tq = min(ATT_Q_BLOCK, s)

    n_blocks = s // tq

    def scores(n, slot):
        r = pl.multiple_of(n * tq, tq)
        for m, qm_ref in enumerate((q1_ref, q2_ref)):
            sc_ref[slot, m] = lax.dot_general(qm_ref[pl.ds(r, tq), :], kb_ref[...], _NT,
                                              preferred_element_type=F32)

    def finish(n, slot):
        r = pl.multiple_of(n * tq, tq)
        e, inv = [], []
        for m in range(2):
            sc = sc_ref[slot, m]
            em = jnp.exp2(sc - jnp.max(sc, axis=-1, keepdims=True))
            e.append(em.astype(BF16))
            inv.append(1.0 / jnp.sum(em, axis=-1, keepdims=True))
        a = e[0] * inv[0].astype(BF16) - e[1] * (lam * inv[1]).astype(BF16)
        o = jnp.dot(a, vb_ref[...], preferred_element_type=F32)
        y = _rms_norm_rows(o, og_ref[...]) * out_scale
        o_ref[0, pl.ds(r, tq), :] = y.astype(o_ref.dtype)

    def scores_ahead(n, slot):
        scores(n, slot)
        tail = sc_ref[slot, 1, tq - 16:, s - HEAD_DIM:]
        zero = (pltpu.bitcast(tail, jnp.uint32) >> 16) >> 16
        vb_ref[0:16, :] = vb_ref[0:16, :] + zero.astype(F32).astype(BF16)

    assert n_blocks == 1 or n_blocks % 2 == 0, n_blocks
    scores(0, 0)
    if n_blocks > 1:
        def pair(m, carry):
            scores_ahead(2 * m + 1, 1)
            finish(2 * m, 0)
            scores_ahead(2 * m + 2, 0)
            finish(2 * m + 1, 1)
            return carry

        lax.fori_loop(0, n_blocks // 2 - 1, pair, 0)
        scores_ahead(n_blocks - 1, 1)
        finish(n_blocks - 2, 0)
        finish(n_blocks - 1, 1)
    else:
        finish(0, 0)


def _attn(p3, cos_t, sin_t, lam, q_gain, k_gain, out_gain, col0, width, out_scale):
    b, s, _ = p3.shape
    heads = width // HEAD_DIM
    blk0 = col0 // HEAD_DIM

    def sect(k):
        return pl.BlockSpec((1, s, HEAD_DIM), lambda bi, hi, k=k: (bi, 0, blk0 + k * heads + hi))

    def tiled_gain(g):
        half = ATT_MAP_DIM // 2
        return jnp.concatenate([g[:half], g[:half], g[half:], g[half:]]).reshape(1, HEAD_DIM)

    row = pl.BlockSpec((1, HEAD_DIM), lambda bi, hi: (0, 0))
    table = pl.BlockSpec((1, s, HEAD_DIM), lambda bi, hi: (bi, 0, 0))
    return pl.pallas_call(
        functools.partial(_attn_body, out_scale=out_scale),
        out_shape=jax.ShapeDtypeStruct((b, s, width), BF16),
        grid=(b, heads),
        in_specs=[
            pl.BlockSpec(memory_space=pltpu.SMEM),
            sect(0), sect(1), sect(2), table, table, row, row,
            pl.BlockSpec((1, HEAD_DIM), lambda bi, hi: (0, hi)),
        ],
        out_specs=pl.BlockSpec((1, s, HEAD_DIM), lambda bi, hi: (bi, 0, hi)),
        scratch_shapes=[pltpu.VMEM((s, HEAD_DIM), BF16) for _ in range(4)]
        + [pltpu.VMEM((2, 2, min(ATT_Q_BLOCK, s), s), F32)],
        compiler_params=_params("parallel", "parallel"),
        name="attn",
    )(lam.reshape(1), p3, p3, p3, cos_t, sin_t, tiled_gain(q_gain), tiled_gain(k_gain),
      out_gain.reshape(1, width))


def _conv_body(b_ref, c_ref, u_ref, w_ref, bias_ref, gain_ref, o_ref):
    s = u_ref.shape[1]
    v = c_ref[0] * u_ref[0]
    row = lax.broadcasted_iota(jnp.int32, (s, 1), 0)
    prev = jnp.where(row == 0, 0.0, pltpu.roll(v, 1, 0))
    nxt = jnp.where(row == s - 1, 0.0, pltpu.roll(v, s - 1, 0))
    y = bias_ref[...] + prev * w_ref[0:1, :]
    y = y + v * w_ref[1:2, :]
    y = y + nxt * w_ref[2:3, :]
    y = b_ref[0] * y
    o_ref[0] = _rms_norm_rows(y, gain_ref[...]).astype(o_ref.dtype)


def _conv(p3, w, bias, gain, col0, width):
    b, s, _ = p3.shape
    groups = width // HEAD_DIM
    blk0 = col0 // HEAD_DIM

    def sect(k):
        return pl.BlockSpec((1, s, HEAD_DIM), lambda bi, gi, k=k: (bi, 0, blk0 + k * groups + gi))

    return pl.pallas_call(
        _conv_body,
        out_shape=jax.ShapeDtypeStruct((b, s, width), BF16),
        grid=(b, groups),
        in_specs=[
            sect(0), sect(1), sect(2),
            pl.BlockSpec((w.shape[0], HEAD_DIM), lambda bi, gi: (0, gi)),
            pl.BlockSpec((1, HEAD_DIM), lambda bi, gi: (0, gi)),
            pl.BlockSpec((1, HEAD_DIM), lambda bi, gi: (0, gi)),
        ],
        out_specs=pl.BlockSpec((1, s, HEAD_DIM), lambda bi, gi: (bi, 0, gi)),
        compiler_params=_params("parallel", "parallel"),
        name="conv",
    )(p3, p3, p3, w, bias.reshape(1, width), gain.reshape(1, width))


def _proj_out_body(x_ref, ya_ref, yb_ref, yc_ref, wa_ref, wb_ref, wc_ref, o_ref):
    acc = jnp.dot(ya_ref[...], wa_ref[...], preferred_element_type=F32)
    acc = acc + jnp.dot(yb_ref[...], wb_ref[...], preferred_element_type=F32)
    acc = acc + jnp.dot(yc_ref[...], wc_ref[...], preferred_element_type=F32)
    o_ref[...] = x_ref[...] + acc


def _proj_out(x2, ya, yb, yc, w):
    t, d = x2.shape
    wa, wb, wc = ya.shape[1], yb.shape[1], yc.shape[1]
    assert wa % wb == 0 and (wa + wb) % wc == 0, (wa, wb, wc)
    tm = _tile(t, 512, 8)

    def rows(width):
        return pl.BlockSpec((tm, width), lambda i: (i, 0))

    def w_rows(n_rows, row0):
        return pl.BlockSpec((n_rows, d), lambda i, blk=row0 // n_rows: (blk, 0))

    return pl.pallas_call(
        _proj_out_body,
        out_shape=jax.ShapeDtypeStruct((t, d), F32),
        grid=(t // tm,),
        in_specs=[rows(d), rows(wa), rows(wb), rows(wc),
                  w_rows(wa, 0), w_rows(wb, wa), w_rows(wc, wa + wb)],
        out_specs=rows(d),
        compiler_params=_params("parallel"),
        name="proj_out",
    )(x2, ya, yb, yc, w, w, w)


def _cast_cols_body(w_ref, o_ref):
    f = w_ref.shape[2]
    o_ref[0, :, :f] = w_ref[0].astype(o_ref.dtype)
    if o_ref.shape[2] > f:
        o_ref[0, :, f:] = jnp.zeros((o_ref.shape[1], o_ref.shape[2] - f), o_ref.dtype)


def _cast_pad_cols(w, cols):
    depth, r, f = w.shape
    tr = _tile(r, 256, 8)
    return pl.pallas_call(
        _cast_cols_body,
        out_shape=jax.ShapeDtypeStruct((depth, r, cols), BF16),
        grid=(depth, r // tr),
        in_specs=[pl.BlockSpec((1, tr, f), lambda l, i: (l, i, 0))],
        out_specs=pl.BlockSpec((1, tr, cols), lambda l, i: (l, i, 0)),
        compiler_params=_params("parallel", "parallel"),
        name="cast_pad_cols",
    )(w)


def _cast_rows_body(w_ref, o_ref, *, valid_tiles):
    @pl.when(pl.program_id(1) < valid_tiles)
    def _():
        o_ref[...] = w_ref[...].astype(o_ref.dtype)

    @pl.when(pl.program_id(1) >= valid_tiles)
    def _():
        o_ref[...] = jnp.zeros_like(o_ref)


def _cast_pad_rows(w, rows):
    depth, f, c = w.shape
    tr = LANES
    assert f % tr == 0 and rows % tr == 0, (f, rows)
    last = f // tr - 1
    return pl.pallas_call(
        functools.partial(_cast_rows_body, valid_tiles=f // tr),
        out_shape=jax.ShapeDtypeStruct((depth, rows, c), BF16),
        grid=(depth, rows // tr),
        in_specs=[pl.BlockSpec((1, tr, c), lambda l, i: (l, jnp.minimum(i, last), 0))],
        out_specs=pl.BlockSpec((1, tr, c), lambda l, i: (l, i, 0)),
        compiler_params=_params("parallel", "parallel"),
        name="cast_pad_rows",
    )(w)


def _ffn_weights(wg, wu, wd):
    f = wg.shape[-1]
    fp = f + ((-f) % FFN_TILE if f > FFN_TILE else 0)
    return _cast_pad_cols(wg, fp), _cast_pad_cols(wu, fp), _cast_pad_rows(wd, fp)


def _attn_lane_order(w, col0, width):
    depth, d, _ = w.shape
    half = ATT_MAP_DIM // 2
    qk = w[:, :, col0:col0 + 2 * width].reshape(depth, d, 2 * width // HEAD_DIM, 2, 2, half)
    qk = qk.transpose(0, 1, 2, 4, 3, 5).reshape(depth, d, 2 * width)
    return lax.dynamic_update_slice(w, qk, (0, 0, col0))


def kernel(x, positions, ffn1_norm, ffn1_w_gate, ffn1_w_up, ffn1_w_down, mix_norm, w_in, hgrn_lb_logits, hgrn_norm, da_q_norm, da_k_norm, da_lambda_q1, da_lambda_k1, da_lambda_q2, da_lambda_k2, da_out_norm, conv_w, conv_b, conv_norm, w_out, ffn2_norm, ffn2_w_gate, ffn2_w_up, ffn2_w_down):
    b, s, d = x.shape
    depth = w_in.shape[0]
    hg_width = d // 2
    da_width = d // 4
    sc_width = d - hg_width - da_width
    da_col0 = 5 * hg_width
    sc_col0 = da_col0 + 3 * da_width

    half = ATT_MAP_DIM // 2
    inv_freq = ROPE_THETA ** (-jnp.arange(0, ATT_MAP_DIM, 2, dtype=F32) / ATT_MAP_DIM)
    ang = positions.astype(F32)[..., None] * inv_freq
    cos_t = jnp.tile(jnp.cos(ang), (1, 1, HEAD_DIM // half))
    sin_half = jnp.sin(ang)
    sin_t = jnp.concatenate([-sin_half, -sin_half, sin_half, sin_half], axis=-1)

    lb_all = jnp.cumsum(jax.nn.softmax(hgrn_lb_logits.astype(F32), axis=1), axis=1)
    lb_all = lb_all - lb_all[:, :1]

    ffn1_w = _ffn_weights(ffn1_w_gate, ffn1_w_up, ffn1_w_down)
    ffn2_w = _ffn_weights(ffn2_w_gate, ffn2_w_up, ffn2_w_down)
    w_in16 = _attn_lane_order(w_in.astype(BF16), da_col0, da_width)
    w_out16 = w_out.astype(BF16)

    x2 = x.reshape(b * s, d)
    for layer in range(depth):
        x2 = _ffn(x2, ffn1_norm[layer], *(w[layer] for w in ffn1_w))

        p3 = _proj_in(x2, mix_norm[layer], w_in16[layer]).reshape(b, s, -1)

        y_a = _hgrn(p3, lb_all[:, layer], hgrn_norm[layer], hg_width)

        lam_init = 0.8 - 0.6 * math.exp(-0.3 * layer)
        lam = (jnp.exp(jnp.sum(da_lambda_q1[layer].astype(F32) * da_lambda_k1[layer].astype(F32)))
               - jnp.exp(jnp.sum(da_lambda_q2[layer].astype(F32) * da_lambda_k2[layer].astype(F32)))
               + lam_init)
        y_b = _attn(p3, cos_t, sin_t, lam, da_q_norm[layer], da_k_norm[layer], da_out_norm[layer],
                    da_col0, da_width, 1.0 - lam_init)

        y_c = _conv(p3, conv_w[layer], conv_b[layer], conv_norm[layer], sc_col0, sc_width)

        x2 = _proj_out(x2, y_a.reshape(b * s, -1), y_b.reshape(b * s, -1), y_c.reshape(b * s, -1),
                       w_out16[layer])

        x2 = _ffn(x2, ffn2_norm[layer], *(w[layer] for w in ffn2_w))
    return x2.reshape(b, s, d)
```

```python
import functools
import math

import jax
import jax.numpy as jnp
import numpy as np
from jax import lax
from jax.experimental import pallas as pl
from jax.experimental.pallas import tpu as pltpu

F32 = jnp.float32
BF16 = jnp.bfloat16

EPS = 1e-6
ROPE_THETA = 10000.0
LANES = 128
HEAD_DIM = LANES
ATT_MAP_DIM = 64
HGRN_CHUNK = 128
ATT_Q_BLOCK = 256
FFN_TILE = 512
VMEM_LIMIT_BYTES = 56 * 1024 * 1024

_NT = (((1,), (1,)), ((), ()))


def _rms_norm_rows(x, gain):
    ms = jnp.mean(x * x, axis=-1, keepdims=True)
    return x * lax.rsqrt(ms + EPS) * gain


def _tile(n, pref, quantum):
    if n <= pref:
        return n
    t = (pref // quantum) * quantum
    while n % t:
        t -= quantum
    return t


def _params(*semantics):
    return pltpu.CompilerParams(dimension_semantics=semantics, vmem_limit_bytes=VMEM_LIMIT_BYTES)


def _ffn_body(x_ref, gain_ref, wg_ref, wu_ref, wd_ref, o_ref, h_ref):
    j = pl.program_id(1)

    @pl.when(j == 0)
    def _():
        x = x_ref[...]
        h_ref[...] = _rms_norm_rows(x, gain_ref[...]).astype(BF16)
        o_ref[...] = x

    h = h_ref[...]
    g = jnp.dot(h, wg_ref[...], preferred_element_type=F32)
    u = jnp.dot(h, wu_ref[...], preferred_element_type=F32)
    a = (g * jax.nn.sigmoid(g) * (0.5 * u)).astype(BF16)
    o_ref[...] += jnp.dot(a, wd_ref[...], preferred_element_type=F32)


def _ffn(x2, gain, wg, wu, wd):
    t, d = x2.shape
    fp = wg.shape[1]
    tm = _tile(t, 1024, 8)
    tf = _tile(fp, FFN_TILE, LANES)
    return pl.pallas_call(
        _ffn_body,
        out_shape=jax.ShapeDtypeStruct((t, d), F32),
        grid=(t // tm, fp // tf),
        in_specs=[
            pl.BlockSpec((tm, d), lambda i, j: (i, 0)),
            pl.BlockSpec((1, d), lambda i, j: (0, 0)),
            pl.BlockSpec((d, tf), lambda i, j: (0, j)),
            pl.BlockSpec((d, tf), lambda i, j: (0, j)),
            pl.BlockSpec((tf, d), lambda i, j: (j, 0)),
        ],
        out_specs=pl.BlockSpec((tm, d), lambda i, j: (i, 0)),
        scratch_shapes=[pltpu.VMEM((tm, d), BF16)],
        compiler_params=_params("parallel", "arbitrary"),
        name="ffn",
    )(x2, gain.reshape(1, d), wg, wu, wd)


def _proj_in_body(x_ref, gain_ref, w_ref, o_ref, h_ref):
    @pl.when(pl.program_id(1) == 0)
    def _():
        h_ref[...] = _rms_norm_rows(x_ref[...], gain_ref[...]).astype(BF16)

    o_ref[...] = jnp.dot(h_ref[...], w_ref[...], preferred_element_type=F32)


def _proj_in(x2, gain, w):
    t, d = x2.shape
    n = w.shape[1]
    tm = _tile(t, 1024, 8)
    tn = _tile(n, 2048, LANES)
    return pl.pallas_call(
        _proj_in_body,
        out_shape=jax.ShapeDtypeStruct((t, n), F32),
        grid=(t // tm, n // tn),
        in_specs=[
            pl.BlockSpec((tm, d), lambda i, j: (i, 0)),
            pl.BlockSpec((1, d), lambda i, j: (0, 0)),
            pl.BlockSpec((d, tn), lambda i, j: (0, j)),
        ],
        out_specs=pl.BlockSpec((tm, tn), lambda i, j: (i, j)),
        scratch_shapes=[pltpu.VMEM((tm, d), BF16)],
        compiler_params=_params("parallel", "arbitrary"),
        name="proj_in",
    )(x2, gain.reshape(1, d), w)


def _hgrn_constants(c):
    t = np.arange(c)[:, None]
    r = np.arange(c)[None, :]
    masks, small = [], []
    h = c // 2
    while h >= 1:
        start = (t // h) * h
        upper = (t // h) % 2 == 1
        masks.append((t // (2 * h) == r // (2 * h)) & upper & ((r // h) % 2 == 0))
        if h <= 2:
            small.append(np.where(upper, (r >= start) & (r <= t), (r > t) & (r <= start + h - 1)))
        h //= 2
    masks = np.stack(masks).astype(np.float32)
    sums = np.stack([r <= t] + small).astype(np.float32)
    out = []
    for flip in (False, True):
        m = masks[:, ::-1, ::-1] if flip else masks
        s = (sums[:, ::-1, ::-1] if flip else sums).reshape(-1, c)
        out.append(jnp.asarray(np.concatenate([s, s], axis=1), dtype=BF16))
        out.append(jnp.asarray(m, dtype=F32))
    return out


def _neg_abs(d):
    sign = jnp.int32(-2 ** 31)
    return lax.bitcast_convert_type(lax.bitcast_convert_type(d, jnp.int32) | sign, F32)


def _hgrn_level_exponents(cum, cum_ref, reverse):
    c = cum.shape[0]
    xs = []
    h = c // 2
    while h >= 4:
        pieces = []
        for j in range(c // (2 * h)):
            r0 = j * 2 * h
            b = cum_ref[pl.ds(r0 + h if reverse else r0 + h - 1, 1), :]
            if h >= 8:
                lo_rows, hi_rows = cum[r0:r0 + h], cum[r0 + h:r0 + 2 * h]
                pieces.extend((lo_rows - b, b - hi_rows) if reverse else (b - lo_rows, hi_rows - b))
            else:
                pieces.append(_neg_abs(cum[r0:r0 + 2 * h] - b))
        xs.append(jnp.concatenate(pieces, axis=0))
        h //= 2
    return xs


def _hgrn_chunk(q, v, z, lb, sums_ref, masks_ref, state_ref, cum_ref, reverse):
    c = q.shape[0]
    f = lb + (1.0 - lb) * jax.nn.sigmoid(z)
    k = 1.0 - f
    lf = jnp.log2(f)
    hi = lf.astype(BF16)
    lo = (lf - hi.astype(F32)).astype(BF16)
    sums = jnp.dot(sums_ref[...], jnp.concatenate([hi, lo], axis=0), preferred_element_type=F32)
    cum = sums[:c]
    cum_ref[...] = cum
    total_row = 0 if reverse else c - 1
    x_key = cum_ref[pl.ds(total_row, 1), :] - cum
    x_levels = _hgrn_level_exponents(cum, cum_ref, reverse) + [sums[c:2 * c], sums[2 * c:]]
    e_levels = [jnp.exp2(x).astype(BF16) for x in x_levels]
    e_q_all = jnp.exp2(cum)
    e_k_all = jnp.exp2(x_key).astype(BF16)
    q16, k16, v16 = q.astype(BF16), k.astype(BF16), v.astype(BF16)
    outs = []
    for a in range(q.shape[1] // HEAD_DIM):
        sl = slice(a * HEAD_DIM, (a + 1) * HEAD_DIM)
        qa, ka, va = q16[:, sl], k16[:, sl], v16[:, sl]
        ka_t = ka.T
        scores = jnp.zeros((c, c), F32)
        for lvl, e_l in enumerate(e_levels):
            el = e_l[:, sl]
            s_l = jnp.dot(qa * el, ka_t * el.T, preferred_element_type=F32)
            scores = scores + s_l * masks_ref[lvl]
        e_q = e_q_all[:, sl]
        state_t = state_ref[a]
        o = jnp.dot(scores.astype(BF16), va, preferred_element_type=F32)
        o = o + jnp.sum(q[:, sl] * k[:, sl], axis=-1, keepdims=True) * v[:, sl]
        o = o + lax.dot_general(qa * e_q.astype(BF16), state_t.astype(BF16), _NT,
                                preferred_element_type=F32)
        state_ref[a] = state_t * e_q[total_row:total_row + 1, :] + jnp.dot(
            v[:, sl].T.astype(BF16), ka * e_k_all[:, sl], preferred_element_type=F32)
        outs.append(o)
    return jnp.concatenate(outs, axis=-1)


def _hgrn_body(q_ref, v_ref, zf_ref, zb_ref, g_ref, lb_ref, gain_ref,
               sums_f_ref, masks_f_ref, sums_b_ref, masks_b_ref,
               o_ref, of_ref, ob_ref, state_f_ref, state_b_ref, cum_ref):
    c = masks_f_ref.shape[1]
    n_chunks = q_ref.shape[1] // c
    per_step = cum_ref.shape[0] // 2
    state_f_ref[...] = jnp.zeros_like(state_f_ref)
    state_b_ref[...] = jnp.zeros_like(state_b_ref)

    def scan_step(n, carry):
        for u in range(per_step):
            m = n * per_step + u
            rf = pl.multiple_of(m * c, c)
            rb = pl.multiple_of((n_chunks - 1 - m) * c, c)
            of_ref[pl.ds(rf, c), :] = _hgrn_chunk(
                q_ref[0, pl.ds(rf, c), :], v_ref[0, pl.ds(rf, c), :], zf_ref[0, pl.ds(rf, c), :],
                lb_ref[0:1, :], sums_f_ref, masks_f_ref, state_f_ref, cum_ref.at[2 * u], False)
            ob_ref[pl.ds(rb, c), :] = _hgrn_chunk(
                q_ref[0, pl.ds(rb, c), :], v_ref[0, pl.ds(rb, c), :], zb_ref[0, pl.ds(rb, c), :],
                lb_ref[1:2, :], sums_b_ref, masks_b_ref, state_b_ref, cum_ref.at[2 * u + 1], True)
        return carry

    lax.fori_loop(0, n_chunks // per_step, scan_step, 0)

    def finish_step(n, carry):
        r = pl.multiple_of(n * c, c)
        o = of_ref[pl.ds(r, c), :] + ob_ref[pl.ds(r, c), :]
        g = g_ref[0, pl.ds(r, c), :]
        normed = [
            _rms_norm_rows(o[:, a * HEAD_DIM:(a + 1) * HEAD_DIM],
                           gain_ref[:, a * HEAD_DIM:(a + 1) * HEAD_DIM])
            for a in range(o.shape[1] // HEAD_DIM)
        ]
        y = jnp.concatenate(normed, axis=-1) * (g * jax.nn.sigmoid(g))
        o_ref[0, pl.ds(r, c), :] = y.astype(o_ref.dtype)
        return carry

    lax.fori_loop(0, n_chunks, finish_step, 0)


def _hgrn(p3, lb, gain, width):
    b, s, _ = p3.shape
    heads = width // HEAD_DIM
    nh = 2 if heads % 2 == 0 else 1
    wb = nh * HEAD_DIM
    n_blk = width // wb
    c = min(HGRN_CHUNK, s)
    sums_f, masks_f, sums_b, masks_b = _hgrn_constants(c)
    per_step = 4 if (s // c) % 4 == 0 else 1

    def sect(k):
        return pl.BlockSpec((1, s, wb), lambda bi, hi, k=k: (bi, 0, k * n_blk + hi))

    def whole(arr):
        return pl.BlockSpec(arr.shape, lambda bi, hi, nd=arr.ndim: (0,) * nd)

    return pl.pallas_call(
        _hgrn_body,
        out_shape=jax.ShapeDtypeStruct((b, s, width), BF16),
        grid=(b, n_blk),
        in_specs=[
            sect(0), sect(1), sect(2), sect(3), sect(4),
            pl.BlockSpec((2, wb), lambda bi, hi: (0, hi)),
            pl.BlockSpec((1, wb), lambda bi, hi: (0, hi)),
            whole(sums_f), whole(masks_f), whole(sums_b), whole(masks_b),
        ],
        out_specs=pl.BlockSpec((1, s, wb), lambda bi, hi: (bi, 0, hi)),
        scratch_shapes=[
            pltpu.VMEM((s, wb), F32), pltpu.VMEM((s, wb), F32),
            pltpu.VMEM((nh, HEAD_DIM, HEAD_DIM), F32), pltpu.VMEM((nh, HEAD_DIM, HEAD_DIM), F32),
            pltpu.VMEM((2 * per_step, c, wb), F32),
        ],
        compiler_params=_params("parallel", "parallel"),
        name="hgrn",
    )(p3, p3, p3, p3, p3, lb, gain.reshape(1, width), sums_f, masks_f, sums_b, masks_b)


def _attn_body(lam_ref, q_ref, k_ref, v_ref, cos_ref, sin_ref, qg_ref, kg_ref, og_ref,
               o_ref, q1_ref, q2_ref, kb_ref, vb_ref, sc_ref, *, out_scale):
    s = q_ref.shape[1]
    quarter = ATT_MAP_DIM // 2

    def lane_map(shape, axis):
        return (lax.broadcasted_iota(jnp.int32, shape, axis) // quarter) % 2

    first_map = lane_map((1, HEAD_DIM), 1) == 0
    square = (HEAD_DIM, HEAD_DIM)
    same_map = jnp.where(lane_map(square, 0) == lane_map(square, 1), 1.0, 0.0).astype(BF16)
    cos = cos_ref[0]
    sin = sin_ref[0]

    def prep(t, gain):
        sq = t * t
        hi = sq.astype(BF16)
        lo = (sq - hi.astype(F32)).astype(BF16)
        ms = (jnp.dot(hi, same_map, preferred_element_type=F32)
              + jnp.dot(lo, same_map, preferred_element_type=F32)) * (1.0 / ATT_MAP_DIM)
        y = t * lax.rsqrt(ms + EPS) * gain
        return y * cos + pltpu.roll(y, HEAD_DIM // 2, 1) * sin

    qr = prep(q_ref[0], qg_ref[...]) * (ATT_MAP_DIM ** -0.5 * math.log2(math.e))
    q1_ref[...] = jnp.where(first_map, qr, 0.0).astype(BF16)
    q2_ref[...] = jnp.where(first_map, 0.0, qr).astype(BF16)
    kb_ref[...] = prep(k_ref[0], kg_ref[...]).astype(BF16)
    vb_ref[...] = v_ref[0].astype(BF16)
    lam = lam_ref[0]
    tq = min(ATT_Q_BLOCK, s)

    n_blocks = s // tq

    def scores(n, slot):
        r = pl.multiple_of(n * tq, tq)
        for m, qm_ref in enumerate((q1_ref, q2_ref)):
            sc_ref[slot, m] = lax.dot_general(qm_ref[pl.ds(r, tq), :], kb_ref[...], _NT,
                                              preferred_element_type=F32)

    def finish(n, slot):
        r = pl.multiple_of(n * tq, tq)
        e, inv = [], []
        for m in range(2):
            sc = sc_ref[slot, m]
            em = jnp.exp2(sc - jnp.max(sc, axis=-1, keepdims=True))
            e.append(em.astype(BF16))
            inv.append(1.0 / jnp.sum(em, axis=-1, keepdims=True))
        a = e[0] * inv[0].astype(BF16) - e[1] * (lam * inv[1]).astype(BF16)
        o = jnp.dot(a, vb_ref[...], preferred_element_type=F32)
        y = _rms_norm_rows(o, og_ref[...]) * out_scale
        o_ref[0, pl.ds(r, tq), :] = y.astype(o_ref.dtype)

    def scores_ahead(n, slot):
        scores(n, slot)
        tail = sc_ref[slot, 1, tq - 16:, s - HEAD_DIM:]
        zero = (pltpu.bitcast(tail, jnp.uint32) >> 16) >> 16
        vb_ref[0:16, :] = vb_ref[0:16, :] + zero.astype(F32).astype(BF16)

    assert n_blocks == 1 or n_blocks % 2 == 0, n_blocks
    scores(0, 0)
    if n_blocks > 1:
        def pair(m, carry):
            scores_ahead(2 * m + 1, 1)
            finish(2 * m, 0)
            scores_ahead(2 * m + 2, 0)
            finish(2 * m + 1, 1)
            return carry

        lax.fori_loop(0, n_blocks // 2 - 1, pair, 0)
        scores_ahead(n_blocks - 1, 1)
        finish(n_blocks - 2, 0)
        finish(n_blocks - 1, 1)
    else:
        finish(0, 0)


def _attn(p3, cos_t, sin_t, lam, q_gain, k_gain, out_gain, col0, width, out_scale):
    b, s, _ = p3.shape
    heads = width // HEAD_DIM
    blk0 = col0 // HEAD_DIM

    def sect(k):
        return pl.BlockSpec((1, s, HEAD_DIM), lambda bi, hi, k=k: (bi, 0, blk0 + k * heads + hi))

    def tiled_gain(g):
        half = ATT_MAP_DIM // 2
        return jnp.concatenate([g[:half], g[:half], g[half:], g[half:]]).reshape(1, HEAD_DIM)

    row = pl.BlockSpec((1, HEAD_DIM), lambda bi, hi: (0, 0))
    table = pl.BlockSpec((1, s, HEAD_DIM), lambda bi, hi: (bi, 0, 0))
    return pl.pallas_call(
        functools.partial(_attn_body, out_scale=out_scale),
        out_shape=jax.ShapeDtypeStruct((b, s, width), BF16),
        grid=(b, heads),
        in_specs=[
            pl.BlockSpec(memory_space=pltpu.SMEM),
            sect(0), sect(1), sect(2), table, table, row, row,
            pl.BlockSpec((1, HEAD_DIM), lambda bi, hi: (0, hi)),
        ],
        out_specs=pl.BlockSpec((1, s, HEAD_DIM), lambda bi, hi: (bi, 0, hi)),
        scratch_shapes=[pltpu.VMEM((s, HEAD_DIM), BF16) for _ in range(4)]
        + [pltpu.VMEM((2, 2, min(ATT_Q_BLOCK, s), s), F32)],
        compiler_params=_params("parallel", "parallel"),
        name="attn",
    )(lam.reshape(1), p3, p3, p3, cos_t, sin_t, tiled_gain(q_gain), tiled_gain(k_gain),
      out_gain.reshape(1, width))


def _conv_body(b_ref, c_ref, u_ref, w_ref, bias_ref, gain_ref, o_ref):
    s = u_ref.shape[1]
    v = c_ref[0] * u_ref[0]
    row = lax.broadcasted_iota(jnp.int32, (s, 1), 0)
    prev = jnp.where(row == 0, 0.0, pltpu.roll(v, 1, 0))
    nxt = jnp.where(row == s - 1, 0.0, pltpu.roll(v, s - 1, 0))
    y = bias_ref[...] + prev * w_ref[0:1, :]
    y = y + v * w_ref[1:2, :]
    y = y + nxt * w_ref[2:3, :]
    y = b_ref[0] * y
    o_ref[0] = _rms_norm_rows(y, gain_ref[...]).astype(o_ref.dtype)


def _conv(p3, w, bias, gain, col0, width):
    b, s, _ = p3.shape
    groups = width // HEAD_DIM
    blk0 = col0 // HEAD_DIM

    def sect(k):
        return pl.BlockSpec((1, s, HEAD_DIM), lambda bi, gi, k=k: (bi, 0, blk0 + k * groups + gi))

    return pl.pallas_call(
        _conv_body,
        out_shape=jax.ShapeDtypeStruct((b, s, width), BF16),
        grid=(b, groups),
        in_specs=[
            sect(0), sect(1), sect(2),
            pl.BlockSpec((w.shape[0], HEAD_DIM), lambda bi, gi: (0, gi)),
            pl.BlockSpec((1, HEAD_DIM), lambda bi, gi: (0, gi)),
            pl.BlockSpec((1, HEAD_DIM), lambda bi, gi: (0, gi)),
        ],
        out_specs=pl.BlockSpec((1, s, HEAD_DIM), lambda bi, gi: (bi, 0, gi)),
        compiler_params=_params("parallel", "parallel"),
        name="conv",
    )(p3, p3, p3, w, bias.reshape(1, width), gain.reshape(1, width))


def _proj_out_body(x_ref, ya_ref, yb_ref, yc_ref, wa_ref, wb_ref, wc_ref, o_ref):
    acc = jnp.dot(ya_ref[...], wa_ref[...], preferred_element_type=F32)
    acc = acc + jnp.dot(yb_ref[...], wb_ref[...], preferred_element_type=F32)
    acc = acc + jnp.dot(yc_ref[...], wc_ref[...], preferred_element_type=F32)
    o_ref[...] = x_ref[...] + acc


def _proj_out(x2, ya, yb, yc, w):
    t, d = x2.shape
    wa, wb, wc = ya.shape[1], yb.shape[1], yc.shape[1]
    assert wa % wb == 0 and (wa + wb) % wc == 0, (wa, wb, wc)
    tm = _tile(t, 512, 8)

    def rows(width):
        return pl.BlockSpec((tm, width), lambda i: (i, 0))

    def w_rows(n_rows, row0):
        return pl.BlockSpec((n_rows, d), lambda i, blk=row0 // n_rows: (blk, 0))

    return pl.pallas_call(
        _proj_out_body,
        out_shape=jax.ShapeDtypeStruct((t, d), F32),
        grid=(t // tm,),
        in_specs=[rows(d), rows(wa), rows(wb), rows(wc),
                  w_rows(wa, 0), w_rows(wb, wa), w_rows(wc, wa + wb)],
        out_specs=rows(d),
        compiler_params=_params("parallel"),
        name="proj_out",
    )(x2, ya, yb, yc, w, w, w)


def _cast_cols_body(w_ref, o_ref):
    f = w_ref.shape[1]
    o_ref[:, :f] = w_ref[...].astype(o_ref.dtype)
    if o_ref.shape[1] > f:
        o_ref[:, f:] = jnp.zeros((o_ref.shape[0], o_ref.shape[1] - f), o_ref.dtype)


def _cast_pad_cols(w, layer, cols):
    _, r, f = w.shape
    tr = _tile(r, 256, 8)
    return pl.pallas_call(
        _cast_cols_body,
        out_shape=jax.ShapeDtypeStruct((r, cols), BF16),
        grid=(r // tr,),
        in_specs=[pl.BlockSpec((None, tr, f), lambda i: (layer, i, 0))],
        out_specs=pl.BlockSpec((tr, cols), lambda i: (i, 0)),
        compiler_params=_params("parallel"),
        name="cast_pad_cols",
    )(w)


def _cast_rows_body(w_ref, o_ref, *, valid_rows):
    tr = o_ref.shape[0]
    row = pl.program_id(0) * tr + lax.broadcasted_iota(jnp.int32, (tr, 1), 0)
    o_ref[...] = jnp.where(row < valid_rows, w_ref[...], 0.0).astype(o_ref.dtype)


def _cast_pad_rows(w, layer, rows):
    _, f, c = w.shape
    tr = _tile(rows, FFN_TILE, 8)
    assert (rows - f) < tr, (rows, f, tr)
    return pl.pallas_call(
        functools.partial(_cast_rows_body, valid_rows=f),
        out_shape=jax.ShapeDtypeStruct((rows, c), BF16),
        grid=(rows // tr,),
        in_specs=[pl.BlockSpec((None, tr, c), lambda i: (layer, i, 0))],
        out_specs=pl.BlockSpec((tr, c), lambda i: (i, 0)),
        compiler_params=_params("parallel"),
        name="cast_pad_rows",
    )(w)


def _cast_body(w_ref, o_ref, *, reorder_tile):
    @pl.when(pl.program_id(0) != reorder_tile)
    def _():
        o_ref[...] = w_ref[...].astype(o_ref.dtype)

    @pl.when(pl.program_id(0) == reorder_tile)
    def _():
        quarter = lax.broadcasted_iota(jnp.int32, (1, HEAD_DIM), 1) // (ATT_MAP_DIM // 2)
        for blk in range(w_ref.shape[1] // HEAD_DIM):
            lanes = slice(blk * HEAD_DIM, (blk + 1) * HEAD_DIM)
            t = w_ref[:, lanes]
            t = jnp.where(quarter == 1, pltpu.roll(t, HEAD_DIM - ATT_MAP_DIM // 2, 1),
                          jnp.where(quarter == 2, pltpu.roll(t, ATT_MAP_DIM // 2, 1), t))
            o_ref[:, lanes] = t.astype(o_ref.dtype)


def _cast(w, layer, reorder_cols=None):
    _, r, n = w.shape
    if reorder_cols is None:
        tn, reorder_tile = _tile(n, 1024, LANES), -1
    else:
        start, tn = reorder_cols
        assert start % tn == 0 and n % tn == 0, (start, tn, n)
        reorder_tile = start // tn
    return pl.pallas_call(
        functools.partial(_cast_body, reorder_tile=reorder_tile),
        out_shape=jax.ShapeDtypeStruct((r, n), BF16),
        grid=(n // tn,),
        in_specs=[pl.BlockSpec((None, r, tn), lambda j: (layer, 0, j))],
        out_specs=pl.BlockSpec((r, tn), lambda j: (0, j)),
        compiler_params=_params("parallel"),
        name="cast",
    )(w)


def _ffn_weights(wg, wu, wd, layer):
    f = wg.shape[-1]
    fp = f + ((-f) % FFN_TILE if f > FFN_TILE else 0)
    return _cast_pad_cols(wg, layer, fp), _cast_pad_cols(wu, layer, fp), _cast_pad_rows(wd, layer, fp)


def kernel(x, positions, ffn1_norm, ffn1_w_gate, ffn1_w_up, ffn1_w_down, mix_norm, w_in, hgrn_lb_logits, hgrn_norm, da_q_norm, da_k_norm, da_lambda_q1, da_lambda_k1, da_lambda_q2, da_lambda_k2, da_out_norm, conv_w, conv_b, conv_norm, w_out, ffn2_norm, ffn2_w_gate, ffn2_w_up, ffn2_w_down):
    b, s, d = x.shape
    depth = w_in.shape[0]
    hg_width = d // 2
    da_width = d // 4
    sc_width = d - hg_width - da_width
    da_col0 = 5 * hg_width
    sc_col0 = da_col0 + 3 * da_width

    half = ATT_MAP_DIM // 2
    inv_freq = ROPE_THETA ** (-jnp.arange(0, ATT_MAP_DIM, 2, dtype=F32) / ATT_MAP_DIM)
    ang = positions.astype(F32)[..., None] * inv_freq
    cos_t = jnp.tile(jnp.cos(ang), (1, 1, HEAD_DIM // half))
    sin_half = jnp.sin(ang)
    sin_t = jnp.concatenate([-sin_half, -sin_half, sin_half, sin_half], axis=-1)

    lb_all = jnp.cumsum(jax.nn.softmax(hgrn_lb_logits.astype(F32), axis=1), axis=1)
    lb_all = lb_all - lb_all[:, :1]

    x2 = x.reshape(b * s, d)
    for layer in range(depth):
        x2 = _ffn(x2, ffn1_norm[layer], *_ffn_weights(ffn1_w_gate, ffn1_w_up, ffn1_w_down, layer))

        w_in_l = _cast(w_in, layer, reorder_cols=(da_col0, 2 * da_width))
        p3 = _proj_in(x2, mix_norm[layer], w_in_l).reshape(b, s, -1)

        y_a = _hgrn(p3, lb_all[:, layer], hgrn_norm[layer], hg_width)

        lam_init = 0.8 - 0.6 * math.exp(-0.3 * layer)
        lam = (jnp.exp(jnp.sum(da_lambda_q1[layer].astype(F32) * da_lambda_k1[layer].astype(F32)))
               - jnp.exp(jnp.sum(da_lambda_q2[layer].astype(F32) * da_lambda_k2[layer].astype(F32)))
               + lam_init)
        y_b = _attn(p3, cos_t, sin_t, lam, da_q_norm[layer], da_k_norm[layer], da_out_norm[layer],
                    da_col0, da_width, 1.0 - lam_init)

        y_c = _conv(p3, conv_w[layer], conv_b[layer], conv_norm[layer], sc_col0, sc_width)

        x2 = _proj_out(x2, y_a.reshape(b * s, -1), y_b.reshape(b * s, -1), y_c.reshape(b * s, -1),
                       _cast(w_out, layer))

        x2 = _ffn(x2, ffn2_norm[layer], *_ffn_weights(ffn2_w_gate, ffn2_w_up, ffn2_w_down, layer))
    return x2.reshape(b, s, d)
```

```python
import functools
import math

import jax
import jax.numpy as jnp
import numpy as np
from jax import lax
from jax.experimental import pallas as pl
from jax.experimental.pallas import tpu as pltpu

F32 = jnp.float32
BF16 = jnp.bfloat16

EPS = 1e-6
ROPE_THETA = 10000.0
LANES = 128
HEAD_DIM = LANES
ATT_MAP_DIM = 64
HGRN_CHUNK = 128
ATT_Q_BLOCK = 256
FFN_TILE = 512
VMEM_LIMIT_BYTES = 56 * 1024 * 1024

_NT = (((1,), (1,)), ((), ()))


def _rms_norm_rows(x, gain):
    ms = jnp.mean(x * x, axis=-1, keepdims=True)
    return x * lax.rsqrt(ms + EPS) * gain


def _tile(n, pref, quantum):
    if n <= pref:
        return n
    t = (pref // quantum) * quantum
    while n % t:
        t -= quantum
    return t


def _params(*semantics):
    return pltpu.CompilerParams(dimension_semantics=semantics, vmem_limit_bytes=VMEM_LIMIT_BYTES)


def _ffn_body(x_ref, gain_ref, wg_ref, wu_ref, wd_ref, o_ref, h_ref):
    j = pl.program_id(1)

    @pl.when(j == 0)
    def _():
        x = x_ref[...]
        h_ref[...] = _rms_norm_rows(x, gain_ref[...]).astype(BF16)
        o_ref[...] = x

    h = h_ref[...]
    g = jnp.dot(h, wg_ref[...], preferred_element_type=F32)
    u = jnp.dot(h, wu_ref[...], preferred_element_type=F32)
    a = (g * jax.nn.sigmoid(g) * (0.5 * u)).astype(BF16)
    o_ref[...] += jnp.dot(a, wd_ref[...], preferred_element_type=F32)


def _ffn(x2, gain, wg, wu, wd):
    t, d = x2.shape
    fp = wg.shape[1]
    tm = _tile(t, 1024, 8)
    tf = _tile(fp, FFN_TILE, LANES)
    return pl.pallas_call(
        _ffn_body,
        out_shape=jax.ShapeDtypeStruct((t, d), F32),
        grid=(t // tm, fp // tf),
        in_specs=[
            pl.BlockSpec((tm, d), lambda i, j: (i, 0)),
            pl.BlockSpec((1, d), lambda i, j: (0, 0)),
            pl.BlockSpec((d, tf), lambda i, j: (0, j)),
            pl.BlockSpec((d, tf), lambda i, j: (0, j)),
            pl.BlockSpec((tf, d), lambda i, j: (j, 0)),
        ],
        out_specs=pl.BlockSpec((tm, d), lambda i, j: (i, 0)),
        scratch_shapes=[pltpu.VMEM((tm, d), BF16)],
        compiler_params=_params("parallel", "arbitrary"),
        name="ffn",
    )(x2, gain.reshape(1, d), wg, wu, wd)


def _proj_in_body(x_ref, gain_ref, w_ref, o_ref, h_ref):
    @pl.when(pl.program_id(1) == 0)
    def _():
        h_ref[...] = _rms_norm_rows(x_ref[...], gain_ref[...]).astype(BF16)

    o_ref[...] = jnp.dot(h_ref[...], w_ref[...], preferred_element_type=F32)


def _proj_in(x2, gain, w):
    t, d = x2.shape
    n = w.shape[1]
    tm = _tile(t, 1024, 8)
    tn = _tile(n, 2048, LANES)
    return pl.pallas_call(
        _proj_in_body,
        out_shape=jax.ShapeDtypeStruct((t, n), F32),
        grid=(t // tm, n // tn),
        in_specs=[
            pl.BlockSpec((tm, d), lambda i, j: (i, 0)),
            pl.BlockSpec((1, d), lambda i, j: (0, 0)),
            pl.BlockSpec((d, tn), lambda i, j: (0, j)),
        ],
        out_specs=pl.BlockSpec((tm, tn), lambda i, j: (i, j)),
        scratch_shapes=[pltpu.VMEM((tm, d), BF16)],
        compiler_params=_params("parallel", "arbitrary"),
        name="proj_in",
    )(x2, gain.reshape(1, d), w)


def _hgrn_constants(c):
    t = np.arange(c)[:, None]
    r = np.arange(c)[None, :]
    masks, small = [], []
    h = c // 2
    while h >= 1:
        start = (t // h) * h
        upper = (t // h) % 2 == 1
        masks.append((t // (2 * h) == r // (2 * h)) & upper & ((r // h) % 2 == 0))
        if h <= 2:
            small.append(np.where(upper, (r >= start) & (r <= t), (r > t) & (r <= start + h - 1)))
        h //= 2
    masks = np.stack(masks).astype(np.float32)
    sums = np.stack([r <= t] + small).astype(np.float32)
    out = []
    for flip in (False, True):
        m = masks[:, ::-1, ::-1] if flip else masks
        s = (sums[:, ::-1, ::-1] if flip else sums).reshape(-1, c)
        out.append(jnp.asarray(np.concatenate([s, s], axis=1), dtype=BF16))
        out.append(jnp.asarray(m, dtype=F32))
    return out


def _neg_abs(d):
    sign = jnp.int32(-2 ** 31)
    return lax.bitcast_convert_type(lax.bitcast_convert_type(d, jnp.int32) | sign, F32)


def _hgrn_level_exponents(cum, cum_ref, reverse):
    c = cum.shape[0]
    xs = []
    h = c // 2
    while h >= 4:
        pieces = []
        for j in range(c // (2 * h)):
            r0 = j * 2 * h
            b = cum_ref[pl.ds(r0 + h if reverse else r0 + h - 1, 1), :]
            if h >= 8:
                lo_rows, hi_rows = cum[r0:r0 + h], cum[r0 + h:r0 + 2 * h]
                pieces.extend((lo_rows - b, b - hi_rows) if reverse else (b - lo_rows, hi_rows - b))
            else:
                pieces.append(_neg_abs(cum[r0:r0 + 2 * h] - b))
        xs.append(jnp.concatenate(pieces, axis=0))
        h //= 2
    return xs


def _hgrn_chunk(q, v, z, lb, sums_ref, masks_ref, state_ref, cum_ref, reverse):
    c = q.shape[0]
    f = lb + (1.0 - lb) * jax.nn.sigmoid(z)
    k = 1.0 - f
    lf = jnp.log2(f)
    hi = lf.astype(BF16)
    lo = (lf - hi.astype(F32)).astype(BF16)
    sums = jnp.dot(sums_ref[...], jnp.concatenate([hi, lo], axis=0), preferred_element_type=F32)
    cum = sums[:c]
    cum_ref[...] = cum
    total_row = 0 if reverse else c - 1
    x_key = cum_ref[pl.ds(total_row, 1), :] - cum
    x_levels = _hgrn_level_exponents(cum, cum_ref, reverse) + [sums[c:2 * c], sums[2 * c:]]
    e_levels = [jnp.exp2(x).astype(BF16) for x in x_levels]
    e_q_all = jnp.exp2(cum)
    e_k_all = jnp.exp2(x_key).astype(BF16)
    q16, k16, v16 = q.astype(BF16), k.astype(BF16), v.astype(BF16)
    outs = []
    for a in range(q.shape[1] // HEAD_DIM):
        sl = slice(a * HEAD_DIM, (a + 1) * HEAD_DIM)
        qa, ka, va = q16[:, sl], k16[:, sl], v16[:, sl]
        ka_t = ka.T
        scores = jnp.zeros((c, c), F32)
        for lvl, e_l in enumerate(e_levels):
            el = e_l[:, sl]
            s_l = jnp.dot(qa * el, ka_t * el.T, preferred_element_type=F32)
            scores = scores + s_l * masks_ref[lvl]
        e_q = e_q_all[:, sl]
        state_t = state_ref[a]
        o = jnp.dot(scores.astype(BF16), va, preferred_element_type=F32)
        o = o + jnp.sum(q[:, sl] * k[:, sl], axis=-1, keepdims=True) * v[:, sl]
        o = o + lax.dot_general(qa * e_q.astype(BF16), state_t.astype(BF16), _NT,
                                preferred_element_type=F32)
        state_ref[a] = state_t * e_q[total_row:total_row + 1, :] + jnp.dot(
            v[:, sl].T.astype(BF16), ka * e_k_all[:, sl], preferred_element_type=F32)
        outs.append(o)
    return jnp.concatenate(outs, axis=-1)


def _hgrn_body(q_ref, v_ref, zf_ref, zb_ref, g_ref, lb_ref, gain_ref,
               sums_f_ref, masks_f_ref, sums_b_ref, masks_b_ref,
               o_ref, of_ref, ob_ref, state_f_ref, state_b_ref, cum_ref):
    c = masks_f_ref.shape[1]
    n_chunks = q_ref.shape[1] // c
    per_step = cum_ref.shape[0] // 2
    state_f_ref[...] = jnp.zeros_like(state_f_ref)
    state_b_ref[...] = jnp.zeros_like(state_b_ref)

    def finish(r, o):
        g = g_ref[0, pl.ds(r, c), :]
        normed = [
            _rms_norm_rows(o[:, a * HEAD_DIM:(a + 1) * HEAD_DIM],
                           gain_ref[:, a * HEAD_DIM:(a + 1) * HEAD_DIM])
            for a in range(o.shape[1] // HEAD_DIM)
        ]
        y = jnp.concatenate(normed, axis=-1) * (g * jax.nn.sigmoid(g))
        o_ref[0, pl.ds(r, c), :] = y.astype(o_ref.dtype)

    def scan_step(n, carry, *, other_half_done):
        for u in range(per_step):
            m = n * per_step + u
            rf = pl.multiple_of(m * c, c)
            rb = pl.multiple_of((n_chunks - 1 - m) * c, c)
            o_f = _hgrn_chunk(
                q_ref[0, pl.ds(rf, c), :], v_ref[0, pl.ds(rf, c), :], zf_ref[0, pl.ds(rf, c), :],
                lb_ref[0:1, :], sums_f_ref, masks_f_ref, state_f_ref, cum_ref.at[2 * u], False)
            o_b = _hgrn_chunk(
                q_ref[0, pl.ds(rb, c), :], v_ref[0, pl.ds(rb, c), :], zb_ref[0, pl.ds(rb, c), :],
                lb_ref[1:2, :], sums_b_ref, masks_b_ref, state_b_ref, cum_ref.at[2 * u + 1], True)
            if other_half_done:
                finish(rf, o_f + ob_ref[pl.ds(rf, c), :])
                finish(rb, of_ref[pl.ds(rb, c), :] + o_b)
            else:
                of_ref[pl.ds(rf, c), :] = o_f
                ob_ref[pl.ds(rb, c), :] = o_b
        return carry

    n_steps = n_chunks // per_step
    if n_steps % 2 == 0:
        lax.fori_loop(0, n_steps // 2, functools.partial(scan_step, other_half_done=False), 0)
        lax.fori_loop(n_steps // 2, n_steps, functools.partial(scan_step, other_half_done=True), 0)
    else:
        lax.fori_loop(0, n_steps, functools.partial(scan_step, other_half_done=False), 0)

        def finish_step(n, carry):
            r = pl.multiple_of(n * c, c)
            finish(r, of_ref[pl.ds(r, c), :] + ob_ref[pl.ds(r, c), :])
            return carry

        lax.fori_loop(0, n_chunks, finish_step, 0)


def _hgrn(p3, lb, gain, width):
    b, s, _ = p3.shape
    heads = width // HEAD_DIM
    nh = 2 if heads % 2 == 0 else 1
    wb = nh * HEAD_DIM
    n_blk = width // wb
    c = min(HGRN_CHUNK, s)
    sums_f, masks_f, sums_b, masks_b = _hgrn_constants(c)
    per_step = 4 if (s // c) % 4 == 0 else 1

    def sect(k):
        return pl.BlockSpec((1, s, wb), lambda bi, hi, k=k: (bi, 0, k * n_blk + hi))

    def whole(arr):
        return pl.BlockSpec(arr.shape, lambda bi, hi, nd=arr.ndim: (0,) * nd)

    return pl.pallas_call(
        _hgrn_body,
        out_shape=jax.ShapeDtypeStruct((b, s, width), BF16),
        grid=(b, n_blk),
        in_specs=[
            sect(0), sect(1), sect(2), sect(3), sect(4),
            pl.BlockSpec((2, wb), lambda bi, hi: (0, hi)),
            pl.BlockSpec((1, wb), lambda bi, hi: (0, hi)),
            whole(sums_f), whole(masks_f), whole(sums_b), whole(masks_b),
        ],
        out_specs=pl.BlockSpec((1, s, wb), lambda bi, hi: (bi, 0, hi)),
        scratch_shapes=[
            pltpu.VMEM((s, wb), F32), pltpu.VMEM((s, wb), F32),
            pltpu.VMEM((nh, HEAD_DIM, HEAD_DIM), F32), pltpu.VMEM((nh, HEAD_DIM, HEAD_DIM), F32),
            pltpu.VMEM((2 * per_step, c, wb), F32),
        ],
        compiler_params=_params("parallel", "parallel"),
        name="hgrn",
    )(p3, p3, p3, p3, p3, lb, gain.reshape(1, width), sums_f, masks_f, sums_b, masks_b)


def _attn_body(lam_ref, q_ref, k_ref, v_ref, cos_ref, sin_ref, qg_ref, kg_ref, og_ref,
               o_ref, q1_ref, q2_ref, kb_ref, vb_ref, sc_ref, *, out_scale):
    s = q_ref.shape[1]
    quarter = ATT_MAP_DIM // 2

    def lane_map(shape, axis):
        return (lax.broadcasted_iota(jnp.int32, shape, axis) // quarter) % 2

    first_map = lane_map((1, HEAD_DIM), 1) == 0
    square = (HEAD_DIM, HEAD_DIM)
    same_map = jnp.where(lane_map(square, 0) == lane_map(square, 1), 1.0, 0.0).astype(BF16)
    cos = cos_ref[0]
    sin = sin_ref[0]

    def prep(t, gain):
        sq = t * t
        hi = sq.astype(BF16)
        lo = (sq - hi.astype(F32)).astype(BF16)
        ms = (jnp.dot(hi, same_map, preferred_element_type=F32)
              + jnp.dot(lo, same_map, preferred_element_type=F32)) * (1.0 / ATT_MAP_DIM)
        y = t * lax.rsqrt(ms + EPS) * gain
        return y * cos + pltpu.roll(y, HEAD_DIM // 2, 1) * sin

    qr = prep(q_ref[0], qg_ref[...]) * (ATT_MAP_DIM ** -0.5 * math.log2(math.e))
    q1_ref[...] = jnp.where(first_map, qr, 0.0).astype(BF16)
    q2_ref[...] = jnp.where(first_map, 0.0, qr).astype(BF16)
    kb_ref[...] = prep(k_ref[0], kg_ref[...]).astype(BF16)
    vb_ref[...] = v_ref[0].astype(BF16)
    lam = lam_ref[0]
    tq = min(ATT_Q_BLOCK, s)

    n_blocks = s // tq

    def scores(n, slot):
        r = pl.multiple_of(n * tq, tq)
        for m, qm_ref in enumerate((q1_ref, q2_ref)):
            sc_ref[slot, m] = lax.dot_general(qm_ref[pl.ds(r, tq), :], kb_ref[...], _NT,
                                              preferred_element_type=F32)

    def finish(n, slot):
        r = pl.multiple_of(n * tq, tq)
        e, inv = [], []
        for m in range(2):
            sc = sc_ref[slot, m]
            em = jnp.exp2(sc - jnp.max(sc, axis=-1, keepdims=True))
            e.append(em.astype(BF16))
            inv.append(1.0 / jnp.sum(em, axis=-1, keepdims=True))
        a = e[0] * inv[0].astype(BF16) - e[1] * (lam * inv[1]).astype(BF16)
        o = jnp.dot(a, vb_ref[...], preferred_element_type=F32)
        y = _rms_norm_rows(o, og_ref[...]) * out_scale
        o_ref[0, pl.ds(r, tq), :] = y.astype(o_ref.dtype)

    def scores_ahead(n, slot):
        scores(n, slot)
        tail = sc_ref[slot, 1, tq - 16:, s - HEAD_DIM:]
        zero = (pltpu.bitcast(tail, jnp.uint32) >> 16) >> 16
        vb_ref[0:16, :] = vb_ref[0:16, :] + zero.astype(F32).astype(BF16)

    assert n_blocks == 1 or n_blocks % 2 == 0, n_blocks
    scores(0, 0)
    if n_blocks > 1:
        def pair(m, carry):
            scores_ahead(2 * m + 1, 1)
            finish(2 * m, 0)
            scores_ahead(2 * m + 2, 0)
            finish(2 * m + 1, 1)
            return carry

        lax.fori_loop(0, n_blocks // 2 - 1, pair, 0)
        scores_ahead(n_blocks - 1, 1)
        finish(n_blocks - 2, 0)
        finish(n_blocks - 1, 1)
    else:
        finish(0, 0)


def _attn(p3, cos_t, sin_t, lam, q_gain, k_gain, out_gain, col0, width, out_scale):
    b, s, _ = p3.shape
    heads = width // HEAD_DIM
    blk0 = col0 // HEAD_DIM

    def sect(k):
        return pl.BlockSpec((1, s, HEAD_DIM), lambda bi, hi, k=k: (bi, 0, blk0 + k * heads + hi))

    def tiled_gain(g):
        half = ATT_MAP_DIM // 2
        return jnp.concatenate([g[:half], g[:half], g[half:], g[half:]]).reshape(1, HEAD_DIM)

    row = pl.BlockSpec((1, HEAD_DIM), lambda bi, hi: (0, 0))
    table = pl.BlockSpec((1, s, HEAD_DIM), lambda bi, hi: (bi, 0, 0))
    return pl.pallas_call(
        functools.partial(_attn_body, out_scale=out_scale),
        out_shape=jax.ShapeDtypeStruct((b, s, width), BF16),
        grid=(b, heads),
        in_specs=[
            pl.BlockSpec(memory_space=pltpu.SMEM),
            sect(0), sect(1), sect(2), table, table, row, row,
            pl.BlockSpec((1, HEAD_DIM), lambda bi, hi: (0, hi)),
        ],
        out_specs=pl.BlockSpec((1, s, HEAD_DIM), lambda bi, hi: (bi, 0, hi)),
        scratch_shapes=[pltpu.VMEM((s, HEAD_DIM), BF16) for _ in range(4)]
        + [pltpu.VMEM((2, 2, min(ATT_Q_BLOCK, s), s), F32)],
        compiler_params=_params("parallel", "parallel"),
        name="attn",
    )(lam.reshape(1), p3, p3, p3, cos_t, sin_t, tiled_gain(q_gain), tiled_gain(k_gain),
      out_gain.reshape(1, width))


def _conv_body(b_ref, c_ref, u_ref, w_ref, bias_ref, gain_ref, o_ref):
    s = u_ref.shape[1]
    v = c_ref[0] * u_ref[0]
    row = lax.broadcasted_iota(jnp.int32, (s, 1), 0)
    prev = jnp.where(row == 0, 0.0, pltpu.roll(v, 1, 0))
    nxt = jnp.where(row == s - 1, 0.0, pltpu.roll(v, s - 1, 0))
    y = bias_ref[...] + prev * w_ref[0:1, :]
    y = y + v * w_ref[1:2, :]
    y = y + nxt * w_ref[2:3, :]
    y = b_ref[0] * y
    for grp in range(y.shape[1] // HEAD_DIM):
        lanes = slice(grp * HEAD_DIM, (grp + 1) * HEAD_DIM)
        o_ref[0, :, lanes] = _rms_norm_rows(y[:, lanes], gain_ref[:, lanes]).astype(o_ref.dtype)


def _conv(p3, w, bias, gain, col0, width):
    b, s, _ = p3.shape
    assert col0 % width == 0, (col0, width)
    blk0 = col0 // width

    def sect(k):
        return pl.BlockSpec((1, s, width), lambda bi, k=k: (bi, 0, blk0 + k))

    def whole(n_rows):
        return pl.BlockSpec((n_rows, width), lambda bi: (0, 0))

    return pl.pallas_call(
        _conv_body,
        out_shape=jax.ShapeDtypeStruct((b, s, width), BF16),
        grid=(b,),
        in_specs=[sect(0), sect(1), sect(2), whole(w.shape[0]), whole(1), whole(1)],
        out_specs=pl.BlockSpec((1, s, width), lambda bi: (bi, 0, 0)),
        compiler_params=_params("parallel"),
        name="conv",
    )(p3, p3, p3, w, bias.reshape(1, width), gain.reshape(1, width))


def _proj_out_body(x_ref, ya_ref, yb_ref, yc_ref, wa_ref, wb_ref, wc_ref, o_ref):
    acc = jnp.dot(ya_ref[...], wa_ref[...], preferred_element_type=F32)
    acc = acc + jnp.dot(yb_ref[...], wb_ref[...], preferred_element_type=F32)
    acc = acc + jnp.dot(yc_ref[...], wc_ref[...], preferred_element_type=F32)
    o_ref[...] = x_ref[...] + acc


def _proj_out(x2, ya, yb, yc, w):
    t, d = x2.shape
    wa, wb, wc = ya.shape[1], yb.shape[1], yc.shape[1]
    assert wa % wb == 0 and (wa + wb) % wc == 0, (wa, wb, wc)
    tm = _tile(t, 512, 8)

    def rows(width):
        return pl.BlockSpec((tm, width), lambda i: (i, 0))

    def w_rows(n_rows, row0):
        return pl.BlockSpec((n_rows, d), lambda i, blk=row0 // n_rows: (blk, 0))

    return pl.pallas_call(
        _proj_out_body,
        out_shape=jax.ShapeDtypeStruct((t, d), F32),
        grid=(t // tm,),
        in_specs=[rows(d), rows(wa), rows(wb), rows(wc),
                  w_rows(wa, 0), w_rows(wb, wa), w_rows(wc, wa + wb)],
        out_specs=rows(d),
        compiler_params=_params("parallel"),
        name="proj_out",
    )(x2, ya, yb, yc, w, w, w)


def _cast_cols_body(w_ref, o_ref):
    f = w_ref.shape[1]
    o_ref[:, :f] = w_ref[...].astype(o_ref.dtype)
    if o_ref.shape[1] > f:
        o_ref[:, f:] = jnp.zeros((o_ref.shape[0], o_ref.shape[1] - f), o_ref.dtype)


def _cast_pad_cols(w, layer, cols):
    _, r, f = w.shape
    tr = _tile(r, 256, 8)
    return pl.pallas_call(
        _cast_cols_body,
        out_shape=jax.ShapeDtypeStruct((r, cols), BF16),
        grid=(r // tr,),
        in_specs=[pl.BlockSpec((None, tr, f), lambda i: (layer, i, 0))],
        out_specs=pl.BlockSpec((tr, cols), lambda i: (i, 0)),
        compiler_params=_params("parallel"),
        name="cast_pad_cols",
    )(w)


def _cast_rows_body(w_ref, o_ref, *, valid_rows):
    tr = o_ref.shape[0]
    row = pl.program_id(0) * tr + lax.broadcasted_iota(jnp.int32, (tr, 1), 0)
    o_ref[...] = jnp.where(row < valid_rows, w_ref[...], 0.0).astype(o_ref.dtype)


def _cast_pad_rows(w, layer, rows):
    _, f, c = w.shape
    tr = _tile(rows, FFN_TILE, 8)
    assert (rows - f) < tr, (rows, f, tr)
    return pl.pallas_call(
        functools.partial(_cast_rows_body, valid_rows=f),
        out_shape=jax.ShapeDtypeStruct((rows, c), BF16),
        grid=(rows // tr,),
        in_specs=[pl.BlockSpec((None, tr, c), lambda i: (layer, i, 0))],
        out_specs=pl.BlockSpec((tr, c), lambda i: (i, 0)),
        compiler_params=_params("parallel"),
        name="cast_pad_rows",
    )(w)


def _cast_body(w_ref, o_ref, *, reorder_tile):
    @pl.when(pl.program_id(0) != reorder_tile)
    def _():
        o_ref[...] = w_ref[...].astype(o_ref.dtype)

    @pl.when(pl.program_id(0) == reorder_tile)
    def _():
        quarter = lax.broadcasted_iota(jnp.int32, (1, HEAD_DIM), 1) // (ATT_MAP_DIM // 2)
        for blk in range(w_ref.shape[1] // HEAD_DIM):
            lanes = slice(blk * HEAD_DIM, (blk + 1) * HEAD_DIM)
            t = w_ref[:, lanes]
            t = jnp.where(quarter == 1, pltpu.roll(t, HEAD_DIM - ATT_MAP_DIM // 2, 1),
                          jnp.where(quarter == 2, pltpu.roll(t, ATT_MAP_DIM // 2, 1), t))
            o_ref[:, lanes] = t.astype(o_ref.dtype)


def _cast(w, layer, reorder_cols=None):
    _, r, n = w.shape
    if reorder_cols is None:
        tn, reorder_tile = _tile(n, 1024, LANES), -1
    else:
        start, tn = reorder_cols
        assert start % tn == 0 and n % tn == 0, (start, tn, n)
        reorder_tile = start // tn
    return pl.pallas_call(
        functools.partial(_cast_body, reorder_tile=reorder_tile),
        out_shape=jax.ShapeDtypeStruct((r, n), BF16),
        grid=(n // tn,),
        in_specs=[pl.BlockSpec((None, r, tn), lambda j: (layer, 0, j))],
        out_specs=pl.BlockSpec((r, tn), lambda j: (0, j)),
        compiler_params=_params("parallel"),
        name="cast",
    )(w)


def _ffn_weights(wg, wu, wd, layer):
    f = wg.shape[-1]
    fp = f + ((-f) % FFN_TILE if f > FFN_TILE else 0)
    return _cast_pad_cols(wg, layer, fp), _cast_pad_cols(wu, layer, fp), _cast_pad_rows(wd, layer, fp)


def kernel(x, positions, ffn1_norm, ffn1_w_gate, ffn1_w_up, ffn1_w_down, mix_norm, w_in, hgrn_lb_logits, hgrn_norm, da_q_norm, da_k_norm, da_lambda_q1, da_lambda_k1, da_lambda_q2, da_lambda_k2, da_out_norm, conv_w, conv_b, conv_norm, w_out, ffn2_norm, ffn2_w_gate, ffn2_w_up, ffn2_w_down):
    b, s, d = x.shape
    depth = w_in.shape[0]
    hg_width = d // 2
    da_width = d // 4
    sc_width = d - hg_width - da_width
    da_col0 = 5 * hg_width
    sc_col0 = da_col0 + 3 * da_width

    half = ATT_MAP_DIM // 2
    inv_freq = ROPE_THETA ** (-jnp.arange(0, ATT_MAP_DIM, 2, dtype=F32) / ATT_MAP_DIM)
    ang = positions.astype(F32)[..., None] * inv_freq
    cos_t = jnp.tile(jnp.cos(ang), (1, 1, HEAD_DIM // half))
    sin_half = jnp.sin(ang)
    sin_t = jnp.concatenate([-sin_half, -sin_half, sin_half, sin_half], axis=-1)

    lb_all = jnp.cumsum(jax.nn.softmax(hgrn_lb_logits.astype(F32), axis=1), axis=1)
    lb_all = lb_all - lb_all[:, :1]

    x2 = x.reshape(b * s, d)
    for layer in range(depth):
        x2 = _ffn(x2, ffn1_norm[layer], *_ffn_weights(ffn1_w_gate, ffn1_w_up, ffn1_w_down, layer))

        w_in_l = _cast(w_in, layer, reorder_cols=(da_col0, 2 * da_width))
        p3 = _proj_in(x2, mix_norm[layer], w_in_l).reshape(b, s, -1)

        y_a = _hgrn(p3, lb_all[:, layer], hgrn_norm[layer], hg_width)

        lam_init = 0.8 - 0.6 * math.exp(-0.3 * layer)
        lam = (jnp.exp(jnp.sum(da_lambda_q1[layer].astype(F32) * da_lambda_k1[layer].astype(F32)))
               - jnp.exp(jnp.sum(da_lambda_q2[layer].astype(F32) * da_lambda_k2[layer].astype(F32)))
               + lam_init)
        y_b = _attn(p3, cos_t, sin_t, lam, da_q_norm[layer], da_k_norm[layer], da_out_norm[layer],
                    da_col0, da_width, 1.0 - lam_init)

        y_c = _conv(p3, conv_w[layer], conv_b[layer], conv_norm[layer], sc_col0, sc_width)

        x2 = _proj_out(x2, y_a.reshape(b * s, -1), y_b.reshape(b * s, -1), y_c.reshape(b * s, -1),
                       _cast(w_out, layer))

        x2 = _ffn(x2, ffn2_norm[layer], *_ffn_weights(ffn2_w_gate, ffn2_w_up, ffn2_w_down, layer))
    return x2.reshape(b, s, d)
```

```python
import functools
import math

import jax
import jax.numpy as jnp
import numpy as np
from jax import lax
from jax.experimental import pallas as pl
from jax.experimental.pallas import tpu as pltpu

F32 = jnp.float32
BF16 = jnp.bfloat16

EPS = 1e-6
ROPE_THETA = 10000.0
LANES = 128
HEAD_DIM = LANES
ATT_MAP_DIM = 64
HGRN_CHUNK = 128
ATT_Q_BLOCK = 512
FFN_TILE = 512
VMEM_LIMIT_BYTES = 56 * 1024 * 1024

_NT = (((1,), (1,)), ((), ()))


def _rms_norm_rows(x, gain):
    ms = jnp.mean(x * x, axis=-1, keepdims=True)
    return x * lax.rsqrt(ms + EPS) * gain


def _tile(n, pref, quantum):
    if n <= pref:
        return n
    t = (pref // quantum) * quantum
    while n % t:
        t -= quantum
    return t


def _params(*semantics):
    return pltpu.CompilerParams(dimension_semantics=semantics, vmem_limit_bytes=VMEM_LIMIT_BYTES)


def _ffn_body(x_ref, gain_ref, wg_ref, wu_ref, wd_ref, o_ref, h_ref):
    j = pl.program_id(1)

    @pl.when(j == 0)
    def _():
        x = x_ref[...]
        h_ref[...] = _rms_norm_rows(x, gain_ref[...]).astype(BF16)
        o_ref[...] = x

    h = h_ref[...]
    g = jnp.dot(h, wg_ref[...], preferred_element_type=F32)
    u = jnp.dot(h, wu_ref[...], preferred_element_type=F32)
    a = (g * jax.nn.sigmoid(g) * (0.5 * u)).astype(BF16)
    o_ref[...] += jnp.dot(a, wd_ref[...], preferred_element_type=F32)


def _ffn(x2, gain, wg, wu, wd):
    t, d = x2.shape
    fp = wg.shape[1]
    tm = _tile(t, 1024, 8)
    tf = _tile(fp, FFN_TILE, LANES)
    return pl.pallas_call(
        _ffn_body,
        out_shape=jax.ShapeDtypeStruct((t, d), F32),
        grid=(t // tm, fp // tf),
        in_specs=[
            pl.BlockSpec((tm, d), lambda i, j: (i, 0)),
            pl.BlockSpec((1, d), lambda i, j: (0, 0)),
            pl.BlockSpec((d, tf), lambda i, j: (0, j)),
            pl.BlockSpec((d, tf), lambda i, j: (0, j)),
            pl.BlockSpec((tf, d), lambda i, j: (j, 0)),
        ],
        out_specs=pl.BlockSpec((tm, d), lambda i, j: (i, 0)),
        scratch_shapes=[pltpu.VMEM((tm, d), BF16)],
        compiler_params=_params("parallel", "arbitrary"),
        name="ffn",
    )(x2, gain.reshape(1, d), wg, wu, wd)


def _proj_in_body(x_ref, gain_ref, w_ref, o_ref, h_ref):
    @pl.when(pl.program_id(1) == 0)
    def _():
        h_ref[...] = _rms_norm_rows(x_ref[...], gain_ref[...]).astype(BF16)

    o_ref[...] = jnp.dot(h_ref[...], w_ref[...], preferred_element_type=F32)


def _proj_in(x2, gain, w):
    t, d = x2.shape
    n = w.shape[1]
    tm = _tile(t, 1024, 8)
    tn = _tile(n, 2048, LANES)
    return pl.pallas_call(
        _proj_in_body,
        out_shape=jax.ShapeDtypeStruct((t, n), F32),
        grid=(t // tm, n // tn),
        in_specs=[
            pl.BlockSpec((tm, d), lambda i, j: (i, 0)),
            pl.BlockSpec((1, d), lambda i, j: (0, 0)),
            pl.BlockSpec((d, tn), lambda i, j: (0, j)),
        ],
        out_specs=pl.BlockSpec((tm, tn), lambda i, j: (i, j)),
        scratch_shapes=[pltpu.VMEM((tm, d), BF16)],
        compiler_params=_params("parallel", "arbitrary"),
        name="proj_in",
    )(x2, gain.reshape(1, d), w)


def _hgrn_constants(c):
    t = np.arange(c)[:, None]
    r = np.arange(c)[None, :]
    masks, small = [], []
    h = c // 2
    while h >= 1:
        start = (t // h) * h
        upper = (t // h) % 2 == 1
        masks.append((t // (2 * h) == r // (2 * h)) & upper & ((r // h) % 2 == 0))
        if h <= 2:
            small.append(np.where(upper, (r >= start) & (r <= t), (r > t) & (r <= start + h - 1)))
        h //= 2
    masks = np.stack(masks).astype(np.float32)
    sums = np.stack([r <= t] + small).astype(np.float32)
    out = []
    for flip in (False, True):
        m = masks[:, ::-1, ::-1] if flip else masks
        s = (sums[:, ::-1, ::-1] if flip else sums).reshape(-1, c)
        out.append(jnp.asarray(np.concatenate([s, s], axis=1), dtype=BF16))
        out.append(jnp.asarray(m, dtype=F32))
    return out


def _neg_abs(d):
    sign = jnp.int32(-2 ** 31)
    return lax.bitcast_convert_type(lax.bitcast_convert_type(d, jnp.int32) | sign, F32)


def _hgrn_level_exponents(cum, cum_ref, reverse):
    c = cum.shape[0]
    xs = []
    h = c // 2
    while h >= 4:
        pieces = []
        for j in range(c // (2 * h)):
            r0 = j * 2 * h
            b = cum_ref[pl.ds(r0 + h if reverse else r0 + h - 1, 1), :]
            if h >= 8:
                lo_rows, hi_rows = cum[r0:r0 + h], cum[r0 + h:r0 + 2 * h]
                pieces.extend((lo_rows - b, b - hi_rows) if reverse else (b - lo_rows, hi_rows - b))
            else:
                pieces.append(_neg_abs(cum[r0:r0 + 2 * h] - b))
        xs.append(jnp.concatenate(pieces, axis=0))
        h //= 2
    return xs


def _hgrn_chunk(q, v, z, lb, sums_ref, masks_ref, state_ref, cum_ref, reverse):
    c = q.shape[0]
    f = lb + (1.0 - lb) * jax.nn.sigmoid(z)
    k = 1.0 - f
    lf = jnp.log2(f)
    hi = lf.astype(BF16)
    lo = (lf - hi.astype(F32)).astype(BF16)
    sums = jnp.dot(sums_ref[...], jnp.concatenate([hi, lo], axis=0), preferred_element_type=F32)
    cum = sums[:c]
    cum_ref[...] = cum
    total_row = 0 if reverse else c - 1
    x_key = cum_ref[pl.ds(total_row, 1), :] - cum
    x_levels = _hgrn_level_exponents(cum, cum_ref, reverse) + [sums[c:2 * c], sums[2 * c:]]
    e_levels = [jnp.exp2(x).astype(BF16) for x in x_levels]
    e_q_all = jnp.exp2(cum)
    e_k_all = jnp.exp2(x_key).astype(BF16)
    q16, k16, v16 = q.astype(BF16), k.astype(BF16), v.astype(BF16)
    outs = []
    for a in range(q.shape[1] // HEAD_DIM):
        sl = slice(a * HEAD_DIM, (a + 1) * HEAD_DIM)
        qa, ka, va = q16[:, sl], k16[:, sl], v16[:, sl]
        ka_t = ka.T
        scores = jnp.zeros((c, c), F32)
        for lvl, e_l in enumerate(e_levels):
            el = e_l[:, sl]
            s_l = jnp.dot(qa * el, ka_t * el.T, preferred_element_type=F32)
            scores = scores + s_l * masks_ref[lvl]
        e_q = e_q_all[:, sl]
        state_t = state_ref[a]
        o = jnp.dot(scores.astype(BF16), va, preferred_element_type=F32)
        o = o + jnp.sum(q[:, sl] * k[:, sl], axis=-1, keepdims=True) * v[:, sl]
        o = o + lax.dot_general(qa * e_q.astype(BF16), state_t.astype(BF16), _NT,
                                preferred_element_type=F32)
        state_ref[a] = state_t * e_q[total_row:total_row + 1, :] + jnp.dot(
            v[:, sl].T.astype(BF16), ka * e_k_all[:, sl], preferred_element_type=F32)
        outs.append(o)
    return jnp.concatenate(outs, axis=-1)


def _hgrn_body(q_ref, v_ref, zf_ref, zb_ref, g_ref, lb_ref, gain_ref,
               sums_f_ref, masks_f_ref, sums_b_ref, masks_b_ref,
               o_ref, of_ref, ob_ref, state_f_ref, state_b_ref, cum_ref):
    c = masks_f_ref.shape[1]
    n_chunks = q_ref.shape[1] // c
    per_step = cum_ref.shape[0] // 2
    state_f_ref[...] = jnp.zeros_like(state_f_ref)
    state_b_ref[...] = jnp.zeros_like(state_b_ref)

    def finish(r, o):
        g = g_ref[0, pl.ds(r, c), :]
        normed = [
            _rms_norm_rows(o[:, a * HEAD_DIM:(a + 1) * HEAD_DIM],
                           gain_ref[:, a * HEAD_DIM:(a + 1) * HEAD_DIM])
            for a in range(o.shape[1] // HEAD_DIM)
        ]
        y = jnp.concatenate(normed, axis=-1) * (g * jax.nn.sigmoid(g))
        o_ref[0, pl.ds(r, c), :] = y.astype(o_ref.dtype)

    def scan_step(n, carry, *, other_half_done):
        for u in range(per_step):
            m = n * per_step + u
            rf = pl.multiple_of(m * c, c)
            rb = pl.multiple_of((n_chunks - 1 - m) * c, c)
            o_f = _hgrn_chunk(
                q_ref[0, pl.ds(rf, c), :], v_ref[0, pl.ds(rf, c), :], zf_ref[0, pl.ds(rf, c), :],
                lb_ref[0:1, :], sums_f_ref, masks_f_ref, state_f_ref, cum_ref.at[2 * u], False)
            o_b = _hgrn_chunk(
                q_ref[0, pl.ds(rb, c), :], v_ref[0, pl.ds(rb, c), :], zb_ref[0, pl.ds(rb, c), :],
                lb_ref[1:2, :], sums_b_ref, masks_b_ref, state_b_ref, cum_ref.at[2 * u + 1], True)
            if other_half_done:
                finish(rf, o_f + ob_ref[pl.ds(rf, c), :])
                finish(rb, of_ref[pl.ds(rb, c), :] + o_b)
            else:
                of_ref[pl.ds(rf, c), :] = o_f
                ob_ref[pl.ds(rb, c), :] = o_b
        return carry

    n_steps = n_chunks // per_step
    if n_steps % 2 == 0:
        lax.fori_loop(0, n_steps // 2, functools.partial(scan_step, other_half_done=False), 0)
        lax.fori_loop(n_steps // 2, n_steps, functools.partial(scan_step, other_half_done=True), 0)
    else:
        lax.fori_loop(0, n_steps, functools.partial(scan_step, other_half_done=False), 0)

        def finish_step(n, carry):
            r = pl.multiple_of(n * c, c)
            finish(r, of_ref[pl.ds(r, c), :] + ob_ref[pl.ds(r, c), :])
            return carry

        lax.fori_loop(0, n_chunks, finish_step, 0)


def _hgrn(p3, lb, gain, width):
    b, s, _ = p3.shape
    heads = width // HEAD_DIM
    nh = 2 if heads % 2 == 0 else 1
    wb = nh * HEAD_DIM
    n_blk = width // wb
    c = min(HGRN_CHUNK, s)
    sums_f, masks_f, sums_b, masks_b = _hgrn_constants(c)
    per_step = 4 if (s // c) % 4 == 0 else 1

    def sect(k):
        return pl.BlockSpec((1, s, wb), lambda bi, hi, k=k: (bi, 0, k * n_blk + hi))

    def whole(arr):
        return pl.BlockSpec(arr.shape, lambda bi, hi, nd=arr.ndim: (0,) * nd)

    return pl.pallas_call(
        _hgrn_body,
        out_shape=jax.ShapeDtypeStruct((b, s, width), BF16),
        grid=(b, n_blk),
        in_specs=[
            sect(0), sect(1), sect(2), sect(3), sect(4),
            pl.BlockSpec((2, wb), lambda bi, hi: (0, hi)),
            pl.BlockSpec((1, wb), lambda bi, hi: (0, hi)),
            whole(sums_f), whole(masks_f), whole(sums_b), whole(masks_b),
        ],
        out_specs=pl.BlockSpec((1, s, wb), lambda bi, hi: (bi, 0, hi)),
        scratch_shapes=[
            pltpu.VMEM((s, wb), F32), pltpu.VMEM((s, wb), F32),
            pltpu.VMEM((nh, HEAD_DIM, HEAD_DIM), F32), pltpu.VMEM((nh, HEAD_DIM, HEAD_DIM), F32),
            pltpu.VMEM((2 * per_step, c, wb), F32),
        ],
        compiler_params=_params("parallel", "parallel"),
        name="hgrn",
    )(p3, p3, p3, p3, p3, lb, gain.reshape(1, width), sums_f, masks_f, sums_b, masks_b)


def _attn_body(lam_ref, q_ref, k_ref, v_ref, cos_ref, sin_ref, qg_ref, kg_ref, og_ref,
               o_ref, q1_ref, q2_ref, kb_ref, vb_ref, sc_ref, *, out_scale):
    s = q_ref.shape[1]
    quarter = ATT_MAP_DIM // 2

    def lane_map(shape, axis):
        return (lax.broadcasted_iota(jnp.int32, shape, axis) // quarter) % 2

    first_map = lane_map((1, HEAD_DIM), 1) == 0
    square = (HEAD_DIM, HEAD_DIM)
    same_map = jnp.where(lane_map(square, 0) == lane_map(square, 1), 1.0, 0.0).astype(BF16)
    cos = cos_ref[0]
    sin = sin_ref[0]

    def prep(t, gain):
        sq = t * t
        hi = sq.astype(BF16)
        lo = (sq - hi.astype(F32)).astype(BF16)
        ms = (jnp.dot(hi, same_map, preferred_element_type=F32)
              + jnp.dot(lo, same_map, preferred_element_type=F32)) * (1.0 / ATT_MAP_DIM)
        y = t * lax.rsqrt(ms + EPS) * gain
        return y * cos + pltpu.roll(y, HEAD_DIM // 2, 1) * sin

    qr = prep(q_ref[0], qg_ref[...]) * (ATT_MAP_DIM ** -0.5 * math.log2(math.e))
    q1_ref[...] = jnp.where(first_map, qr, 0.0).astype(BF16)
    q2_ref[...] = jnp.where(first_map, 0.0, qr).astype(BF16)
    kb_ref[...] = prep(k_ref[0], kg_ref[...]).astype(BF16)
    vb_ref[...] = v_ref[0].astype(BF16)
    lam = lam_ref[0]
    tq = min(ATT_Q_BLOCK, s)

    n_blocks = s // tq

    def scores(n, slot):
        r = n * tq
        for m, qm_ref in enumerate((q1_ref, q2_ref)):
            sc_ref[slot, m] = lax.dot_general(qm_ref[pl.ds(r, tq), :], kb_ref[...], _NT,
                                              preferred_element_type=F32)

    def finish(n, slot):
        r = n * tq
        e, inv = [], []
        for m in range(2):
            sc = sc_ref[slot, m]
            em = jnp.exp2(sc - jnp.max(sc, axis=-1, keepdims=True))
            e.append(em.astype(BF16))
            inv.append(1.0 / jnp.sum(em, axis=-1, keepdims=True))
        a = e[0] * inv[0].astype(BF16) - e[1] * (lam * inv[1]).astype(BF16)
        o = jnp.dot(a, vb_ref[...], preferred_element_type=F32)
        y = _rms_norm_rows(o, og_ref[...]) * out_scale
        o_ref[0, pl.ds(r, tq), :] = y.astype(o_ref.dtype)

    def scores_ahead(n, slot):
        scores(n, slot)
        tail = sc_ref[slot, 1, tq - 16:, s - HEAD_DIM:]
        zero = (pltpu.bitcast(tail, jnp.uint32) >> 16) >> 16
        vb_ref[0:16, :] = vb_ref[0:16, :] + zero.astype(F32).astype(BF16)

    scores(0, 0)
    for n in range(n_blocks):
        if n + 1 < n_blocks:
            scores_ahead(n + 1, (n + 1) % 2)
        finish(n, n % 2)


def _attn(p3, cos_t, sin_t, lam, q_gain, k_gain, out_gain, col0, width, out_scale):
    b, s, _ = p3.shape
    heads = width // HEAD_DIM
    blk0 = col0 // HEAD_DIM

    def sect(k):
        return pl.BlockSpec((1, s, HEAD_DIM), lambda bi, hi, k=k: (bi, 0, blk0 + k * heads + hi))

    def tiled_gain(g):
        half = ATT_MAP_DIM // 2
        return jnp.concatenate([g[:half], g[:half], g[half:], g[half:]]).reshape(1, HEAD_DIM)

    row = pl.BlockSpec((1, HEAD_DIM), lambda bi, hi: (0, 0))
    table = pl.BlockSpec((1, s, HEAD_DIM), lambda bi, hi: (bi, 0, 0))
    return pl.pallas_call(
        functools.partial(_attn_body, out_scale=out_scale),
        out_shape=jax.ShapeDtypeStruct((b, s, width), BF16),
        grid=(b, heads),
        in_specs=[
            pl.BlockSpec(memory_space=pltpu.SMEM),
            sect(0), sect(1), sect(2), table, table, row, row,
            pl.BlockSpec((1, HEAD_DIM), lambda bi, hi: (0, hi)),
        ],
        out_specs=pl.BlockSpec((1, s, HEAD_DIM), lambda bi, hi: (bi, 0, hi)),
        scratch_shapes=[pltpu.VMEM((s, HEAD_DIM), BF16) for _ in range(4)]
        + [pltpu.VMEM((2, 2, min(ATT_Q_BLOCK, s), s), F32)],
        compiler_params=_params("parallel", "parallel"),
        name="attn",
    )(lam.reshape(1), p3, p3, p3, cos_t, sin_t, tiled_gain(q_gain), tiled_gain(k_gain),
      out_gain.reshape(1, width))


def _conv_body(b_ref, c_ref, u_ref, w_ref, bias_ref, gain_ref, o_ref):
    s = u_ref.shape[1]
    v = c_ref[0] * u_ref[0]
    row = lax.broadcasted_iota(jnp.int32, (s, 1), 0)
    prev = jnp.where(row == 0, 0.0, pltpu.roll(v, 1, 0))
    nxt = jnp.where(row == s - 1, 0.0, pltpu.roll(v, s - 1, 0))
    y = bias_ref[...] + prev * w_ref[0:1, :]
    y = y + v * w_ref[1:2, :]
    y = y + nxt * w_ref[2:3, :]
    y = b_ref[0] * y
    for grp in range(y.shape[1] // HEAD_DIM):
        lanes = slice(grp * HEAD_DIM, (grp + 1) * HEAD_DIM)
        o_ref[0, :, lanes] = _rms_norm_rows(y[:, lanes], gain_ref[:, lanes]).astype(o_ref.dtype)


def _conv(p3, w, bias, gain, col0, width):
    b, s, _ = p3.shape
    assert col0 % width == 0, (col0, width)
    blk0 = col0 // width

    def sect(k):
        return pl.BlockSpec((1, s, width), lambda bi, k=k: (bi, 0, blk0 + k))

    def whole(n_rows):
        return pl.BlockSpec((n_rows, width), lambda bi: (0, 0))

    return pl.pallas_call(
        _conv_body,
        out_shape=jax.ShapeDtypeStruct((b, s, width), BF16),
        grid=(b,),
        in_specs=[sect(0), sect(1), sect(2), whole(w.shape[0]), whole(1), whole(1)],
        out_specs=pl.BlockSpec((1, s, width), lambda bi: (bi, 0, 0)),
        compiler_params=_params("parallel"),
        name="conv",
    )(p3, p3, p3, w, bias.reshape(1, width), gain.reshape(1, width))


def _proj_out_body(x_ref, ya_ref, yb_ref, yc_ref, wa_ref, wb_ref, wc_ref, o_ref):
    acc = jnp.dot(ya_ref[...], wa_ref[...], preferred_element_type=F32)
    acc = acc + jnp.dot(yb_ref[...], wb_ref[...], preferred_element_type=F32)
    acc = acc + jnp.dot(yc_ref[...], wc_ref[...], preferred_element_type=F32)
    o_ref[...] = x_ref[...] + acc


def _proj_out(x2, ya, yb, yc, w):
    t, d = x2.shape
    wa, wb, wc = ya.shape[1], yb.shape[1], yc.shape[1]
    assert wa % wb == 0 and (wa + wb) % wc == 0, (wa, wb, wc)
    tm = _tile(t, 512, 8)

    def rows(width):
        return pl.BlockSpec((tm, width), lambda i: (i, 0))

    def w_rows(n_rows, row0):
        return pl.BlockSpec((n_rows, d), lambda i, blk=row0 // n_rows: (blk, 0))

    return pl.pallas_call(
        _proj_out_body,
        out_shape=jax.ShapeDtypeStruct((t, d), F32),
        grid=(t // tm,),
        in_specs=[rows(d), rows(wa), rows(wb), rows(wc),
                  w_rows(wa, 0), w_rows(wb, wa), w_rows(wc, wa + wb)],
        out_specs=rows(d),
        compiler_params=_params("parallel"),
        name="proj_out",
    )(x2, ya, yb, yc, w, w, w)


def _cast_cols_body(w_ref, o_ref):
    f = w_ref.shape[1]
    o_ref[:, :f] = w_ref[...].astype(o_ref.dtype)
    if o_ref.shape[1] > f:
        o_ref[:, f:] = jnp.zeros((o_ref.shape[0], o_ref.shape[1] - f), o_ref.dtype)


def _cast_pad_cols(w, layer, cols):
    _, r, f = w.shape
    tr = _tile(r, 256, 8)
    return pl.pallas_call(
        _cast_cols_body,
        out_shape=jax.ShapeDtypeStruct((r, cols), BF16),
        grid=(r // tr,),
        in_specs=[pl.BlockSpec((None, tr, f), lambda i: (layer, i, 0))],
        out_specs=pl.BlockSpec((tr, cols), lambda i: (i, 0)),
        compiler_params=_params("parallel"),
        name="cast_pad_cols",
    )(w)


def _cast_rows_body(w_ref, o_ref, *, valid_rows):
    tr = o_ref.shape[0]
    row = pl.program_id(0) * tr + lax.broadcasted_iota(jnp.int32, (tr, 1), 0)
    o_ref[...] = jnp.where(row < valid_rows, w_ref[...], 0.0).astype(o_ref.dtype)


def _cast_pad_rows(w, layer, rows):
    _, f, c = w.shape
    tr = _tile(rows, FFN_TILE, 8)
    assert (rows - f) < tr, (rows, f, tr)
    return pl.pallas_call(
        functools.partial(_cast_rows_body, valid_rows=f),
        out_shape=jax.ShapeDtypeStruct((rows, c), BF16),
        grid=(rows // tr,),
        in_specs=[pl.BlockSpec((None, tr, c), lambda i: (layer, i, 0))],
        out_specs=pl.BlockSpec((tr, c), lambda i: (i, 0)),
        compiler_params=_params("parallel"),
        name="cast_pad_rows",
    )(w)


def _cast_body(w_ref, o_ref, *, reorder_tile):
    @pl.when(pl.program_id(0) != reorder_tile)
    def _():
        o_ref[...] = w_ref[...].astype(o_ref.dtype)

    @pl.when(pl.program_id(0) == reorder_tile)
    def _():
        quarter = lax.broadcasted_iota(jnp.int32, (1, HEAD_DIM), 1) // (ATT_MAP_DIM // 2)
        for blk in range(w_ref.shape[1] // HEAD_DIM):
            lanes = slice(blk * HEAD_DIM, (blk + 1) * HEAD_DIM)
            t = w_ref[:, lanes]
            t = jnp.where(quarter == 1, pltpu.roll(t, HEAD_DIM - ATT_MAP_DIM // 2, 1),
                          jnp.where(quarter == 2, pltpu.roll(t, ATT_MAP_DIM // 2, 1), t))
            o_ref[:, lanes] = t.astype(o_ref.dtype)


def _cast(w, layer, reorder_cols=None):
    _, r, n = w.shape
    if reorder_cols is None:
        tn, reorder_tile = _tile(n, 1024, LANES), -1
    else:
        start, tn = reorder_cols
        assert start % tn == 0 and n % tn == 0, (start, tn, n)
        reorder_tile = start // tn
    return pl.pallas_call(
        functools.partial(_cast_body, reorder_tile=reorder_tile),
        out_shape=jax.ShapeDtypeStruct((r, n), BF16),
        grid=(n // tn,),
        in_specs=[pl.BlockSpec((None, r, tn), lambda j: (layer, 0, j))],
        out_specs=pl.BlockSpec((r, tn), lambda j: (0, j)),
        compiler_params=_params("parallel"),
        name="cast",
    )(w)


def _ffn_weights(wg, wu, wd, layer):
    f = wg.shape[-1]
    fp = f + ((-f) % FFN_TILE if f > FFN_TILE else 0)
    return _cast_pad_cols(wg, layer, fp), _cast_pad_cols(wu, layer, fp), _cast_pad_rows(wd, layer, fp)


def kernel(x, positions, ffn1_norm, ffn1_w_gate, ffn1_w_up, ffn1_w_down, mix_norm, w_in, hgrn_lb_logits, hgrn_norm, da_q_norm, da_k_norm, da_lambda_q1, da_lambda_k1, da_lambda_q2, da_lambda_k2, da_out_norm, conv_w, conv_b, conv_norm, w_out, ffn2_norm, ffn2_w_gate, ffn2_w_up, ffn2_w_down):
    b, s, d = x.shape
    depth = w_in.shape[0]
    hg_width = d // 2
    da_width = d // 4
    sc_width = d - hg_width - da_width
    da_col0 = 5 * hg_width
    sc_col0 = da_col0 + 3 * da_width

    half = ATT_MAP_DIM // 2
    inv_freq = ROPE_THETA ** (-jnp.arange(0, ATT_MAP_DIM, 2, dtype=F32) / ATT_MAP_DIM)
    ang = positions.astype(F32)[..., None] * inv_freq
    cos_t = jnp.tile(jnp.cos(ang), (1, 1, HEAD_DIM // half))
    sin_half = jnp.sin(ang)
    sin_t = jnp.concatenate([-sin_half, -sin_half, sin_half, sin_half], axis=-1)

    lb_all = jnp.cumsum(jax.nn.softmax(hgrn_lb_logits.astype(F32), axis=1), axis=1)
    lb_all = lb_all - lb_all[:, :1]

    x2 = x.reshape(b * s, d)
    for layer in range(depth):
        x2 = _ffn(x2, ffn1_norm[layer], *_ffn_weights(ffn1_w_gate, ffn1_w_up, ffn1_w_down, layer))

        w_in_l = _cast(w_in, layer, reorder_cols=(da_col0, 2 * da_width))
        p3 = _proj_in(x2, mix_norm[layer], w_in_l).reshape(b, s, -1)

        y_a = _hgrn(p3, lb_all[:, layer], hgrn_norm[layer], hg_width)

        lam_init = 0.8 - 0.6 * math.exp(-0.3 * layer)
        lam = (jnp.exp(jnp.sum(da_lambda_q1[layer].astype(F32) * da_lambda_k1[layer].astype(F32)))
               - jnp.exp(jnp.sum(da_lambda_q2[layer].astype(F32) * da_lambda_k2[layer].astype(F32)))
               + lam_init)
        y_b = _attn(p3, cos_t, sin_t, lam, da_q_norm[layer], da_k_norm[layer], da_out_norm[layer],
                    da_col0, da_width, 1.0 - lam_init)

        y_c = _conv(p3, conv_w[layer], conv_b[layer], conv_norm[layer], sc_col0, sc_width)

        x2 = _proj_out(x2, y_a.reshape(b * s, -1), y_b.reshape(b * s, -1), y_c.reshape(b * s, -1),
                       _cast(w_out, layer))

        x2 = _ffn(x2, ffn2_norm[layer], *_ffn_weights(ffn2_w_gate, ffn2_w_up, ffn2_w_down, layer))
    return x2.reshape(b, s, d)
```

```python
import functools
import math

import jax
import jax.numpy as jnp
import numpy as np
from jax import lax
from jax.experimental import pallas as pl
from jax.experimental.pallas import tpu as pltpu

F32 = jnp.float32
BF16 = jnp.bfloat16

EPS = 1e-6
ROPE_THETA = 10000.0
LANES = 128
SUBLANES = 8
BF16_ROWS = 2 * SUBLANES
HEAD_DIM = LANES
ATT_MAP_DIM = 64
HGRN_CHUNK = 128
HGRN_CHUNKS_PER_STEP = 4
ATT_Q_BLOCK = 512
FFN_TILE = 512
FFN_TOKEN_TILE = 1024
PROJ_IN_TOKEN_TILE = 1024
PROJ_IN_COL_TILE = 2048
PROJ_OUT_TOKEN_TILE = 512
CAST_ROW_TILE = 256
CAST_COL_TILE = 1024
VMEM_LIMIT_BYTES = 56 * 1024 * 1024

_NT = (((1,), (1,)), ((), ()))


def _rms_norm_rows(x, gain):
    ms = jnp.mean(x * x, axis=-1, keepdims=True)
    return x * lax.rsqrt(ms + EPS) * gain


def _tile(n, pref, quantum):
    if n <= pref:
        return n
    t = (pref // quantum) * quantum
    while n % t:
        t -= quantum
    return t


def _params(*semantics):
    return pltpu.CompilerParams(dimension_semantics=semantics, vmem_limit_bytes=VMEM_LIMIT_BYTES)


def _ffn_body(x_ref, gain_ref, wg_ref, wu_ref, wd_ref, o_ref, h_ref):
    j = pl.program_id(1)

    @pl.when(j == 0)
    def _():
        x = x_ref[...]
        h_ref[...] = _rms_norm_rows(x, gain_ref[...]).astype(BF16)
        o_ref[...] = x

    h = h_ref[...]
    g = jnp.dot(h, wg_ref[...], preferred_element_type=F32)
    u = jnp.dot(h, wu_ref[...], preferred_element_type=F32)
    a = (g * jax.nn.sigmoid(g) * (0.5 * u)).astype(BF16)
    o_ref[...] += jnp.dot(a, wd_ref[...], preferred_element_type=F32)


def _ffn(x2, gain, wg, wu, wd):
    t, d = x2.shape
    fp = wg.shape[1]
    tm = _tile(t, FFN_TOKEN_TILE, SUBLANES)
    tf = _tile(fp, FFN_TILE, LANES)
    return pl.pallas_call(
        _ffn_body,
        out_shape=jax.ShapeDtypeStruct((t, d), F32),
        grid=(t // tm, fp // tf),
        in_specs=[
            pl.BlockSpec((tm, d), lambda i, j: (i, 0)),
            pl.BlockSpec((1, d), lambda i, j: (0, 0)),
            pl.BlockSpec((d, tf), lambda i, j: (0, j)),
            pl.BlockSpec((d, tf), lambda i, j: (0, j)),
            pl.BlockSpec((tf, d), lambda i, j: (j, 0)),
        ],
        out_specs=pl.BlockSpec((tm, d), lambda i, j: (i, 0)),
        scratch_shapes=[pltpu.VMEM((tm, d), BF16)],
        compiler_params=_params("parallel", "arbitrary"),
        name="ffn",
    )(x2, gain.reshape(1, d), wg, wu, wd)


def _proj_in_body(x_ref, gain_ref, w_ref, o_ref, h_ref):
    @pl.when(pl.program_id(1) == 0)
    def _():
        h_ref[...] = _rms_norm_rows(x_ref[...], gain_ref[...]).astype(BF16)

    o_ref[...] = jnp.dot(h_ref[...], w_ref[...], preferred_element_type=F32)


def _proj_in(x2, gain, w):
    t, d = x2.shape
    n = w.shape[1]
    tm = _tile(t, PROJ_IN_TOKEN_TILE, SUBLANES)
    tn = _tile(n, PROJ_IN_COL_TILE, LANES)
    return pl.pallas_call(
        _proj_in_body,
        out_shape=jax.ShapeDtypeStruct((t, n), F32),
        grid=(t // tm, n // tn),
        in_specs=[
            pl.BlockSpec((tm, d), lambda i, j: (i, 0)),
            pl.BlockSpec((1, d), lambda i, j: (0, 0)),
            pl.BlockSpec((d, tn), lambda i, j: (0, j)),
        ],
        out_specs=pl.BlockSpec((tm, tn), lambda i, j: (i, j)),
        scratch_shapes=[pltpu.VMEM((tm, d), BF16)],
        compiler_params=_params("parallel", "arbitrary"),
        name="proj_in",
    )(x2, gain.reshape(1, d), w)


def _hgrn_constants(c):
    t = np.arange(c)[:, None]
    r = np.arange(c)[None, :]
    masks, small = [], []
    h = c // 2
    while h >= 1:
        start = (t // h) * h
        upper = (t // h) % 2 == 1
        masks.append((t // (2 * h) == r // (2 * h)) & upper & ((r // h) % 2 == 0))
        if h <= 2:
            small.append(np.where(upper, (r >= start) & (r <= t), (r > t) & (r <= start + h - 1)))
        h //= 2
    masks = np.stack(masks).astype(np.float32)
    sums = np.stack([r <= t] + small).astype(np.float32)
    out = []
    for flip in (False, True):
        m = masks[:, ::-1, ::-1] if flip else masks
        s = (sums[:, ::-1, ::-1] if flip else sums).reshape(-1, c)
        out.append(jnp.asarray(np.concatenate([s, s], axis=1), dtype=BF16))
        out.append(jnp.asarray(m, dtype=F32))
    return out


def _neg_abs(d):
    sign = jnp.int32(-2 ** 31)
    return lax.bitcast_convert_type(lax.bitcast_convert_type(d, jnp.int32) | sign, F32)


def _hgrn_level_exponents(cum, cum_ref, reverse):
    c = cum.shape[0]
    xs = []
    h = c // 2
    while h >= 4:
        pieces = []
        for j in range(c // (2 * h)):
            r0 = j * 2 * h
            b = cum_ref[pl.ds(r0 + h if reverse else r0 + h - 1, 1), :]
            if h >= 8:
                lo_rows, hi_rows = cum[r0:r0 + h], cum[r0 + h:r0 + 2 * h]
                pieces.extend((lo_rows - b, b - hi_rows) if reverse else (b - lo_rows, hi_rows - b))
            else:
                pieces.append(_neg_abs(cum[r0:r0 + 2 * h] - b))
        xs.append(jnp.concatenate(pieces, axis=0))
        h //= 2
    return xs


def _hgrn_chunk(q, v, z, lb, sums_ref, masks_ref, state_ref, cum_ref, reverse):
    c = q.shape[0]
    f = lb + (1.0 - lb) * jax.nn.sigmoid(z)
    k = 1.0 - f
    lf = jnp.log2(f)
    hi = lf.astype(BF16)
    lo = (lf - hi.astype(F32)).astype(BF16)
    sums = jnp.dot(sums_ref[...], jnp.concatenate([hi, lo], axis=0), preferred_element_type=F32)
    cum = sums[:c]
    cum_ref[...] = cum
    total_row = 0 if reverse else c - 1
    x_key = cum_ref[pl.ds(total_row, 1), :] - cum
    x_levels = _hgrn_level_exponents(cum, cum_ref, reverse) + [sums[c:2 * c], sums[2 * c:]]
    e_levels = [jnp.exp2(x).astype(BF16) for x in x_levels]
    e_q_all = jnp.exp2(cum)
    e_k_all = jnp.exp2(x_key).astype(BF16)
    q16, k16, v16 = q.astype(BF16), k.astype(BF16), v.astype(BF16)
    outs = []
    for a in range(q.shape[1] // HEAD_DIM):
        sl = slice(a * HEAD_DIM, (a + 1) * HEAD_DIM)
        qa, ka, va = q16[:, sl], k16[:, sl], v16[:, sl]
        ka_t = ka.T
        scores = jnp.zeros((c, c), F32)
        for lvl, e_l in enumerate(e_levels):
            el = e_l[:, sl]
            s_l = jnp.dot(qa * el, ka_t * el.T, preferred_element_type=F32)
            scores = scores + s_l * masks_ref[lvl]
        e_q = e_q_all[:, sl]
        state_t = state_ref[a]
        o = jnp.dot(scores.astype(BF16), va, preferred_element_type=F32)
        o = o + jnp.sum(q[:, sl] * k[:, sl], axis=-1, keepdims=True) * v[:, sl]
        o = o + lax.dot_general(qa * e_q.astype(BF16), state_t.astype(BF16), _NT,
                                preferred_element_type=F32)
        state_ref[a] = state_t * e_q[total_row:total_row + 1, :] + jnp.dot(
            v[:, sl].T.astype(BF16), ka * e_k_all[:, sl], preferred_element_type=F32)
        outs.append(o)
    return jnp.concatenate(outs, axis=-1)


def _hgrn_body(q_ref, v_ref, zf_ref, zb_ref, g_ref, lb_ref, gain_ref,
               sums_f_ref, masks_f_ref, sums_b_ref, masks_b_ref,
               o_ref, of_ref, ob_ref, state_f_ref, state_b_ref, cum_ref):
    c = masks_f_ref.shape[1]
    n_chunks = q_ref.shape[1] // c
    per_step = cum_ref.shape[0] // 2
    state_f_ref[...] = jnp.zeros_like(state_f_ref)
    state_b_ref[...] = jnp.zeros_like(state_b_ref)

    def finish(r, o):
        g = g_ref[0, pl.ds(r, c), :]
        normed = [
            _rms_norm_rows(o[:, a * HEAD_DIM:(a + 1) * HEAD_DIM],
                           gain_ref[:, a * HEAD_DIM:(a + 1) * HEAD_DIM])
            for a in range(o.shape[1] // HEAD_DIM)
        ]
        y = jnp.concatenate(normed, axis=-1) * (g * jax.nn.sigmoid(g))
        o_ref[0, pl.ds(r, c), :] = y.astype(o_ref.dtype)

    def scan_step(n, carry, *, other_half_done):
        for u in range(per_step):
            m = n * per_step + u
            rf = pl.multiple_of(m * c, c)
            rb = pl.multiple_of((n_chunks - 1 - m) * c, c)
            o_f = _hgrn_chunk(
                q_ref[0, pl.ds(rf, c), :], v_ref[0, pl.ds(rf, c), :], zf_ref[0, pl.ds(rf, c), :],
                lb_ref[0:1, :], sums_f_ref, masks_f_ref, state_f_ref, cum_ref.at[2 * u], False)
            o_b = _hgrn_chunk(
                q_ref[0, pl.ds(rb, c), :], v_ref[0, pl.ds(rb, c), :], zb_ref[0, pl.ds(rb, c), :],
                lb_ref[1:2, :], sums_b_ref, masks_b_ref, state_b_ref, cum_ref.at[2 * u + 1], True)
            if other_half_done:
                finish(rf, o_f + ob_ref[pl.ds(rf, c), :])
                finish(rb, of_ref[pl.ds(rb, c), :] + o_b)
            else:
                of_ref[pl.ds(rf, c), :] = o_f
                ob_ref[pl.ds(rb, c), :] = o_b
        return carry

    n_steps = n_chunks // per_step
    if n_steps % 2 == 0:
        lax.fori_loop(0, n_steps // 2, functools.partial(scan_step, other_half_done=False), 0)
        lax.fori_loop(n_steps // 2, n_steps, functools.partial(scan_step, other_half_done=True), 0)
    else:
        lax.fori_loop(0, n_steps, functools.partial(scan_step, other_half_done=False), 0)

        def finish_step(n, carry):
            r = pl.multiple_of(n * c, c)
            finish(r, of_ref[pl.ds(r, c), :] + ob_ref[pl.ds(r, c), :])
            return carry

        lax.fori_loop(0, n_chunks, finish_step, 0)


def _hgrn(p3, lb, gain, width):
    b, s, _ = p3.shape
    heads = width // HEAD_DIM
    nh = 2 if heads % 2 == 0 else 1
    wb = nh * HEAD_DIM
    n_blk = width // wb
    c = min(HGRN_CHUNK, s)
    sums_f, masks_f, sums_b, masks_b = _hgrn_constants(c)
    per_step = HGRN_CHUNKS_PER_STEP if (s // c) % HGRN_CHUNKS_PER_STEP == 0 else 1

    def sect(k):
        return pl.BlockSpec((1, s, wb), lambda bi, hi, k=k: (bi, 0, k * n_blk + hi))

    def whole(arr):
        return pl.BlockSpec(arr.shape, lambda bi, hi, nd=arr.ndim: (0,) * nd)

    return pl.pallas_call(
        _hgrn_body,
        out_shape=jax.ShapeDtypeStruct((b, s, width), BF16),
        grid=(b, n_blk),
        in_specs=[
            sect(0), sect(1), sect(2), sect(3), sect(4),
            pl.BlockSpec((2, wb), lambda bi, hi: (0, hi)),
            pl.BlockSpec((1, wb), lambda bi, hi: (0, hi)),
            whole(sums_f), whole(masks_f), whole(sums_b), whole(masks_b),
        ],
        out_specs=pl.BlockSpec((1, s, wb), lambda bi, hi: (bi, 0, hi)),
        scratch_shapes=[
            pltpu.VMEM((s, wb), F32), pltpu.VMEM((s, wb), F32),
            pltpu.VMEM((nh, HEAD_DIM, HEAD_DIM), F32), pltpu.VMEM((nh, HEAD_DIM, HEAD_DIM), F32),
            pltpu.VMEM((2 * per_step, c, wb), F32),
        ],
        compiler_params=_params("parallel", "parallel"),
        name="hgrn",
    )(p3, p3, p3, p3, p3, lb, gain.reshape(1, width), sums_f, masks_f, sums_b, masks_b)


def _attn_body(lam_ref, q_ref, k_ref, v_ref, cos_ref, sin_ref, qg_ref, kg_ref, og_ref,
               o_ref, q1_ref, q2_ref, kb_ref, vb_ref, sc_ref, *, out_scale):
    s = q_ref.shape[1]
    quarter = ATT_MAP_DIM // 2

    def lane_map(shape, axis):
        return (lax.broadcasted_iota(jnp.int32, shape, axis) // quarter) % 2

    first_map = lane_map((1, HEAD_DIM), 1) == 0
    square = (HEAD_DIM, HEAD_DIM)
    same_map = jnp.where(lane_map(square, 0) == lane_map(square, 1), 1.0, 0.0).astype(BF16)
    cos = cos_ref[0]
    sin = sin_ref[0]

    def prep(t, gain):
        sq = t * t
        hi = sq.astype(BF16)
        lo = (sq - hi.astype(F32)).astype(BF16)
        ms = (jnp.dot(hi, same_map, preferred_element_type=F32)
              + jnp.dot(lo, same_map, preferred_element_type=F32)) * (1.0 / ATT_MAP_DIM)
        y = t * lax.rsqrt(ms + EPS) * gain
        return y * cos + pltpu.roll(y, HEAD_DIM // 2, 1) * sin

    qr = prep(q_ref[0], qg_ref[...]) * (ATT_MAP_DIM ** -0.5 * math.log2(math.e))
    q1_ref[...] = jnp.where(first_map, qr, 0.0).astype(BF16)
    q2_ref[...] = jnp.where(first_map, 0.0, qr).astype(BF16)
    kb_ref[...] = prep(k_ref[0], kg_ref[...]).astype(BF16)
    vb_ref[...] = v_ref[0].astype(BF16)
    lam = lam_ref[0]
    tq = min(ATT_Q_BLOCK, s)

    n_blocks = s // tq

    def scores(n, slot):
        r = n * tq
        for m, qm_ref in enumerate((q1_ref, q2_ref)):
            sc_ref[slot, m] = lax.dot_general(qm_ref[pl.ds(r, tq), :], kb_ref[...], _NT,
                                              preferred_element_type=F32)

    def finish(n, slot):
        r = n * tq
        e, inv = [], []
        for m in range(2):
            sc = sc_ref[slot, m]
            em = jnp.exp2(sc - jnp.max(sc, axis=-1, keepdims=True))
            e.append(em.astype(BF16))
            inv.append(1.0 / jnp.sum(em, axis=-1, keepdims=True))
        a = e[0] * inv[0].astype(BF16) - e[1] * (lam * inv[1]).astype(BF16)
        o = jnp.dot(a, vb_ref[...], preferred_element_type=F32)
        y = _rms_norm_rows(o, og_ref[...]) * out_scale
        o_ref[0, pl.ds(r, tq), :] = y.astype(o_ref.dtype)

    def scores_ahead(n, slot):
        scores(n, slot)
        tail = sc_ref[slot, 1, tq - BF16_ROWS:, s - HEAD_DIM:]
        zero = (pltpu.bitcast(tail, jnp.uint32) >> 16) >> 16
        vb_ref[0:BF16_ROWS, :] = vb_ref[0:BF16_ROWS, :] + zero.astype(F32).astype(BF16)

    scores(0, 0)
    for n in range(n_blocks):
        if n + 1 < n_blocks:
            scores_ahead(n + 1, (n + 1) % 2)
        finish(n, n % 2)


def _attn(p3, cos_t, sin_t, lam, q_gain, k_gain, out_gain, col0, width, out_scale):
    b, s, _ = p3.shape
    heads = width // HEAD_DIM
    blk0 = col0 // HEAD_DIM

    def sect(k):
        return pl.BlockSpec((1, s, HEAD_DIM), lambda bi, hi, k=k: (bi, 0, blk0 + k * heads + hi))

    def tiled_gain(g):
        half = ATT_MAP_DIM // 2
        return jnp.concatenate([g[:half], g[:half], g[half:], g[half:]]).reshape(1, HEAD_DIM)

    row = pl.BlockSpec((1, HEAD_DIM), lambda bi, hi: (0, 0))
    table = pl.BlockSpec((1, s, HEAD_DIM), lambda bi, hi: (bi, 0, 0))
    return pl.pallas_call(
        functools.partial(_attn_body, out_scale=out_scale),
        out_shape=jax.ShapeDtypeStruct((b, s, width), BF16),
        grid=(b, heads),
        in_specs=[
            pl.BlockSpec(memory_space=pltpu.SMEM),
            sect(0), sect(1), sect(2), table, table, row, row,
            pl.BlockSpec((1, HEAD_DIM), lambda bi, hi: (0, hi)),
        ],
        out_specs=pl.BlockSpec((1, s, HEAD_DIM), lambda bi, hi: (bi, 0, hi)),
        scratch_shapes=[pltpu.VMEM((s, HEAD_DIM), BF16) for _ in range(4)]
        + [pltpu.VMEM((2, 2, min(ATT_Q_BLOCK, s), s), F32)],
        compiler_params=_params("parallel", "parallel"),
        name="attn",
    )(lam.reshape(1), p3, p3, p3, cos_t, sin_t, tiled_gain(q_gain), tiled_gain(k_gain),
      out_gain.reshape(1, width))


def _conv_body(b_ref, c_ref, u_ref, w_ref, bias_ref, gain_ref, o_ref):
    s = u_ref.shape[1]
    v = c_ref[0] * u_ref[0]
    row = lax.broadcasted_iota(jnp.int32, (s, 1), 0)
    prev = jnp.where(row == 0, 0.0, pltpu.roll(v, 1, 0))
    nxt = jnp.where(row == s - 1, 0.0, pltpu.roll(v, s - 1, 0))
    y = bias_ref[...] + prev * w_ref[0:1, :]
    y = y + v * w_ref[1:2, :]
    y = y + nxt * w_ref[2:3, :]
    y = b_ref[0] * y
    for grp in range(y.shape[1] // HEAD_DIM):
        lanes = slice(grp * HEAD_DIM, (grp + 1) * HEAD_DIM)
        o_ref[0, :, lanes] = _rms_norm_rows(y[:, lanes], gain_ref[:, lanes]).astype(o_ref.dtype)


def _conv(p3, w, bias, gain, col0, width):
    b, s, _ = p3.shape
    assert col0 % width == 0, (col0, width)
    blk0 = col0 // width

    def sect(k):
        return pl.BlockSpec((1, s, width), lambda bi, k=k: (bi, 0, blk0 + k))

    def whole(n_rows):
        return pl.BlockSpec((n_rows, width), lambda bi: (0, 0))

    return pl.pallas_call(
        _conv_body,
        out_shape=jax.ShapeDtypeStruct((b, s, width), BF16),
        grid=(b,),
        in_specs=[sect(0), sect(1), sect(2), whole(w.shape[0]), whole(1), whole(1)],
        out_specs=pl.BlockSpec((1, s, width), lambda bi: (bi, 0, 0)),
        compiler_params=_params("parallel"),
        name="conv",
    )(p3, p3, p3, w, bias.reshape(1, width), gain.reshape(1, width))


def _proj_out_body(x_ref, ya_ref, yb_ref, yc_ref, wa_ref, wb_ref, wc_ref, o_ref):
    acc = jnp.dot(ya_ref[...], wa_ref[...], preferred_element_type=F32)
    acc = acc + jnp.dot(yb_ref[...], wb_ref[...], preferred_element_type=F32)
    acc = acc + jnp.dot(yc_ref[...], wc_ref[...], preferred_element_type=F32)
    o_ref[...] = x_ref[...] + acc


def _proj_out(x2, ya, yb, yc, w):
    t, d = x2.shape
    wa, wb, wc = ya.shape[1], yb.shape[1], yc.shape[1]
    assert wa % wb == 0 and (wa + wb) % wc == 0, (wa, wb, wc)
    tm = _tile(t, PROJ_OUT_TOKEN_TILE, SUBLANES)

    def rows(width):
        return pl.BlockSpec((tm, width), lambda i: (i, 0))

    def w_rows(n_rows, row0):
        return pl.BlockSpec((n_rows, d), lambda i, blk=row0 // n_rows: (blk, 0))

    return pl.pallas_call(
        _proj_out_body,
        out_shape=jax.ShapeDtypeStruct((t, d), F32),
        grid=(t // tm,),
        in_specs=[rows(d), rows(wa), rows(wb), rows(wc),
                  w_rows(wa, 0), w_rows(wb, wa), w_rows(wc, wa + wb)],
        out_specs=rows(d),
        compiler_params=_params("parallel"),
        name="proj_out",
    )(x2, ya, yb, yc, w, w, w)


def _cast_cols_body(w_ref, o_ref):
    f = w_ref.shape[1]
    o_ref[:, :f] = w_ref[...].astype(o_ref.dtype)
    if o_ref.shape[1] > f:
        o_ref[:, f:] = jnp.zeros((o_ref.shape[0], o_ref.shape[1] - f), o_ref.dtype)


def _cast_pad_cols(w, layer, cols):
    _, r, f = w.shape
    tr = _tile(r, CAST_ROW_TILE, SUBLANES)
    return pl.pallas_call(
        _cast_cols_body,
        out_shape=jax.ShapeDtypeStruct((r, cols), BF16),
        grid=(r // tr,),
        in_specs=[pl.BlockSpec((None, tr, f), lambda i: (layer, i, 0))],
        out_specs=pl.BlockSpec((tr, cols), lambda i: (i, 0)),
        compiler_params=_params("parallel"),
        name="cast_pad_cols",
    )(w)


def _cast_rows_body(w_ref, o_ref, *, valid_rows):
    tr = o_ref.shape[0]
    row = pl.program_id(0) * tr + lax.broadcasted_iota(jnp.int32, (tr, 1), 0)
    o_ref[...] = jnp.where(row < valid_rows, w_ref[...], 0.0).astype(o_ref.dtype)


def _cast_pad_rows(w, layer, rows):
    _, f, c = w.shape
    tr = _tile(rows, FFN_TILE, SUBLANES)
    assert (rows - f) < tr, (rows, f, tr)
    return pl.pallas_call(
        functools.partial(_cast_rows_body, valid_rows=f),
        out_shape=jax.ShapeDtypeStruct((rows, c), BF16),
        grid=(rows // tr,),
        in_specs=[pl.BlockSpec((None, tr, c), lambda i: (layer, i, 0))],
        out_specs=pl.BlockSpec((tr, c), lambda i: (i, 0)),
        compiler_params=_params("parallel"),
        name="cast_pad_rows",
    )(w)


def _cast_body(w_ref, o_ref, *, reorder_tile):
    @pl.when(pl.program_id(0) != reorder_tile)
    def _():
        o_ref[...] = w_ref[...].astype(o_ref.dtype)

    @pl.when(pl.program_id(0) == reorder_tile)
    def _():
        quarter = lax.broadcasted_iota(jnp.int32, (1, HEAD_DIM), 1) // (ATT_MAP_DIM // 2)
        for blk in range(w_ref.shape[1] // HEAD_DIM):
            lanes = slice(blk * HEAD_DIM, (blk + 1) * HEAD_DIM)
            t = w_ref[:, lanes]
            t = jnp.where(quarter == 1, pltpu.roll(t, HEAD_DIM - ATT_MAP_DIM // 2, 1),
                          jnp.where(quarter == 2, pltpu.roll(t, ATT_MAP_DIM // 2, 1), t))
            o_ref[:, lanes] = t.astype(o_ref.dtype)


def _cast(w, layer, reorder_cols=None):
    _, r, n = w.shape
    if reorder_cols is None:
        tn, reorder_tile = _tile(n, CAST_COL_TILE, LANES), -1
    else:
        start, tn = reorder_cols
        assert start % tn == 0 and n % tn == 0, (start, tn, n)
        reorder_tile = start // tn
    return pl.pallas_call(
        functools.partial(_cast_body, reorder_tile=reorder_tile),
        out_shape=jax.ShapeDtypeStruct((r, n), BF16),
        grid=(n // tn,),
        in_specs=[pl.BlockSpec((None, r, tn), lambda j: (layer, 0, j))],
        out_specs=pl.BlockSpec((r, tn), lambda j: (0, j)),
        compiler_params=_params("parallel"),
        name="cast",
    )(w)


def _ffn_weights(wg, wu, wd, layer):
    f = wg.shape[-1]
    fp = f + ((-f) % FFN_TILE if f > FFN_TILE else 0)
    return _cast_pad_cols(wg, layer, fp), _cast_pad_cols(wu, layer, fp), _cast_pad_rows(wd, layer, fp)


def kernel(x, positions, ffn1_norm, ffn1_w_gate, ffn1_w_up, ffn1_w_down, mix_norm, w_in, hgrn_lb_logits, hgrn_norm, da_q_norm, da_k_norm, da_lambda_q1, da_lambda_k1, da_lambda_q2, da_lambda_k2, da_out_norm, conv_w, conv_b, conv_norm, w_out, ffn2_norm, ffn2_w_gate, ffn2_w_up, ffn2_w_down):
    b, s, d = x.shape
    depth = w_in.shape[0]
    hg_width = d // 2
    da_width = d // 4
    sc_width = d - hg_width - da_width
    da_col0 = 5 * hg_width
    sc_col0 = da_col0 + 3 * da_width

    half = ATT_MAP_DIM // 2
    inv_freq = ROPE_THETA ** (-jnp.arange(0, ATT_MAP_DIM, 2, dtype=F32) / ATT_MAP_DIM)
    ang = positions.astype(F32)[..., None] * inv_freq
    cos_t = jnp.tile(jnp.cos(ang), (1, 1, HEAD_DIM // half))
    sin_half = jnp.sin(ang)
    sin_t = jnp.concatenate([-sin_half, -sin_half, sin_half, sin_half], axis=-1)

    lb_all = jnp.cumsum(jax.nn.softmax(hgrn_lb_logits.astype(F32), axis=1), axis=1)
    lb_all = lb_all - lb_all[:, :1]

    x2 = x.reshape(b * s, d)
    for layer in range(depth):
        x2 = _ffn(x2, ffn1_norm[layer], *_ffn_weights(ffn1_w_gate, ffn1_w_up, ffn1_w_down, layer))

        w_in_l = _cast(w_in, layer, reorder_cols=(da_col0, 2 * da_width))
        p3 = _proj_in(x2, mix_norm[layer], w_in_l).reshape(b, s, -1)

        y_a = _hgrn(p3, lb_all[:, layer], hgrn_norm[layer], hg_width)

        lam_init = 0.8 - 0.6 * math.exp(-0.3 * layer)
        lam = (jnp.exp(jnp.sum(da_lambda_q1[layer].astype(F32) * da_lambda_k1[layer].astype(F32)))
               - jnp.exp(jnp.sum(da_lambda_q2[layer].astype(F32) * da_lambda_k2[layer].astype(F32)))
               + lam_init)
        y_b = _attn(p3, cos_t, sin_t, lam, da_q_norm[layer], da_k_norm[layer], da_out_norm[layer],
                    da_col0, da_width, 1.0 - lam_init)

        y_c = _conv(p3, conv_w[layer], conv_b[layer], conv_norm[layer], sc_col0, sc_width)

        x2 = _proj_out(x2, y_a.reshape(b * s, -1), y_b.reshape(b * s, -1), y_c.reshape(b * s, -1),
                       _cast(w_out, layer))

        x2 = _ffn(x2, ffn2_norm[layer], *_ffn_weights(ffn2_w_gate, ffn2_w_up, ffn2_w_down, layer))
    return x2.reshape(b, s, d)
```

```python
import functools
import math

import jax
import jax.numpy as jnp
import numpy as np
from jax import lax
from jax.experimental import pallas as pl
from jax.experimental.pallas import tpu as pltpu

F32 = jnp.float32
BF16 = jnp.bfloat16

EPS = 1e-6
ROPE_THETA = 10000.0
LANES = 128
SUBLANES = 8
BF16_ROWS = 2 * SUBLANES
HEAD_DIM = LANES
ATT_MAP_DIM = 64
HGRN_CHUNK = 128
HGRN_CHUNKS_PER_STEP = 4
ATT_Q_BLOCK = 512
FFN_TILE = 512
FFN_TOKEN_TILE = 1024
PROJ_IN_TOKEN_TILE = 1024
PROJ_IN_COL_TILE = 2048
PROJ_OUT_TOKEN_TILE = 512
CAST_ROW_TILE = 256
CAST_COL_TILE = 1024
VMEM_LIMIT_BYTES = 56 * 1024 * 1024

_NT = (((1,), (1,)), ((), ()))


def _rms_norm_rows(x, gain):
    ms = jnp.mean(x * x, axis=-1, keepdims=True)
    return x * lax.rsqrt(ms + EPS) * gain


def _tile(n, pref, quantum):
    if n <= pref:
        return n
    t = (pref // quantum) * quantum
    while n % t:
        t -= quantum
    return t


def _params(*semantics):
    return pltpu.CompilerParams(dimension_semantics=semantics, vmem_limit_bytes=VMEM_LIMIT_BYTES)


def _ffn_body(x_ref, gain_ref, wg_ref, wu_ref, wd_ref, o_ref, h_ref):
    j = pl.program_id(1)

    @pl.when(j == 0)
    def _():
        x = x_ref[...]
        h_ref[...] = _rms_norm_rows(x, gain_ref[...]).astype(BF16)
        o_ref[...] = x

    h = h_ref[...]
    g = jnp.dot(h, wg_ref[...], preferred_element_type=F32)
    u = jnp.dot(h, wu_ref[...], preferred_element_type=F32)
    a = (g * jax.nn.sigmoid(g) * (0.5 * u)).astype(BF16)
    o_ref[...] += jnp.dot(a, wd_ref[...], preferred_element_type=F32)


def _ffn(x2, gain, wg, wu, wd):
    t, d = x2.shape
    fp = wg.shape[1]
    tm = _tile(t, FFN_TOKEN_TILE, SUBLANES)
    tf = _tile(fp, FFN_TILE, LANES)
    return pl.pallas_call(
        _ffn_body,
        out_shape=jax.ShapeDtypeStruct((t, d), F32),
        grid=(t // tm, fp // tf),
        in_specs=[
            pl.BlockSpec((tm, d), lambda i, j: (i, 0)),
            pl.BlockSpec((1, d), lambda i, j: (0, 0)),
            pl.BlockSpec((d, tf), lambda i, j: (0, j)),
            pl.BlockSpec((d, tf), lambda i, j: (0, j)),
            pl.BlockSpec((tf, d), lambda i, j: (j, 0)),
        ],
        out_specs=pl.BlockSpec((tm, d), lambda i, j: (i, 0)),
        scratch_shapes=[pltpu.VMEM((tm, d), BF16)],
        compiler_params=_params("parallel", "arbitrary"),
        name="ffn",
    )(x2, gain.reshape(1, d), wg, wu, wd)


def _proj_in_body(x_ref, gain_ref, w_ref, o_ref, h_ref):
    @pl.when(pl.program_id(1) == 0)
    def _():
        h_ref[...] = _rms_norm_rows(x_ref[...], gain_ref[...]).astype(BF16)

    o_ref[...] = jnp.dot(h_ref[...], w_ref[...], preferred_element_type=F32).astype(o_ref.dtype)


def _proj_in(x2, gain, w):
    t, d = x2.shape
    n = w.shape[1]
    tm = _tile(t, PROJ_IN_TOKEN_TILE, SUBLANES)
    tn = _tile(n, PROJ_IN_COL_TILE, LANES)
    return pl.pallas_call(
        _proj_in_body,
        out_shape=jax.ShapeDtypeStruct((t, n), BF16),
        grid=(t // tm, n // tn),
        in_specs=[
            pl.BlockSpec((tm, d), lambda i, j: (i, 0)),
            pl.BlockSpec((1, d), lambda i, j: (0, 0)),
            pl.BlockSpec((d, tn), lambda i, j: (0, j)),
        ],
        out_specs=pl.BlockSpec((tm, tn), lambda i, j: (i, j)),
        scratch_shapes=[pltpu.VMEM((tm, d), BF16)],
        compiler_params=_params("parallel", "arbitrary"),
        name="proj_in",
    )(x2, gain.reshape(1, d), w)


def _hgrn_constants(c):
    t = np.arange(c)[:, None]
    r = np.arange(c)[None, :]
    masks, small = [], []
    h = c // 2
    while h >= 1:
        start = (t // h) * h
        upper = (t // h) % 2 == 1
        masks.append((t // (2 * h) == r // (2 * h)) & upper & ((r // h) % 2 == 0))
        if h <= 2:
            small.append(np.where(upper, (r >= start) & (r <= t), (r > t) & (r <= start + h - 1)))
        h //= 2
    masks = np.stack(masks).astype(np.float32)
    sums = np.stack([r <= t] + small).astype(np.float32)
    out = []
    for flip in (False, True):
        m = masks[:, ::-1, ::-1] if flip else masks
        s = (sums[:, ::-1, ::-1] if flip else sums).reshape(-1, c)
        out.append(jnp.asarray(np.concatenate([s, s], axis=1), dtype=BF16))
        out.append(jnp.asarray(m, dtype=F32))
    return out


def _neg_abs(d):
    sign = jnp.int32(-2 ** 31)
    return lax.bitcast_convert_type(lax.bitcast_convert_type(d, jnp.int32) | sign, F32)


def _hgrn_level_exponents(cum, cum_ref, reverse):
    c = cum.shape[0]
    xs = []
    h = c // 2
    while h >= 4:
        pieces = []
        for j in range(c // (2 * h)):
            r0 = j * 2 * h
            b = cum_ref[pl.ds(r0 + h if reverse else r0 + h - 1, 1), :]
            if h >= 8:
                lo_rows, hi_rows = cum[r0:r0 + h], cum[r0 + h:r0 + 2 * h]
                pieces.extend((lo_rows - b, b - hi_rows) if reverse else (b - lo_rows, hi_rows - b))
            else:
                pieces.append(_neg_abs(cum[r0:r0 + 2 * h] - b))
        xs.append(jnp.concatenate(pieces, axis=0))
        h //= 2
    return xs


def _hgrn_chunk(q, v, z, lb, sums_ref, masks_ref, state_ref, cum_ref, reverse):
    c = q.shape[0]
    f = lb + (1.0 - lb) * jax.nn.sigmoid(z)
    k = 1.0 - f
    lf = jnp.log2(f)
    hi = lf.astype(BF16)
    lo = (lf - hi.astype(F32)).astype(BF16)
    sums = jnp.dot(sums_ref[...], jnp.concatenate([hi, lo], axis=0), preferred_element_type=F32)
    cum = sums[:c]
    cum_ref[...] = cum
    total_row = 0 if reverse else c - 1
    x_key = cum_ref[pl.ds(total_row, 1), :] - cum
    x_levels = _hgrn_level_exponents(cum, cum_ref, reverse) + [sums[c:2 * c], sums[2 * c:]]
    e_levels = [jnp.exp2(x).astype(BF16) for x in x_levels]
    e_q_all = jnp.exp2(cum)
    e_k_all = jnp.exp2(x_key).astype(BF16)
    q16, k16, v16 = q.astype(BF16), k.astype(BF16), v.astype(BF16)
    outs = []
    for a in range(q.shape[1] // HEAD_DIM):
        sl = slice(a * HEAD_DIM, (a + 1) * HEAD_DIM)
        qa, ka, va = q16[:, sl], k16[:, sl], v16[:, sl]
        ka_t = ka.T
        scores = jnp.zeros((c, c), F32)
        for lvl, e_l in enumerate(e_levels):
            el = e_l[:, sl]
            s_l = jnp.dot(qa * el, ka_t * el.T, preferred_element_type=F32)
            scores = scores + s_l * masks_ref[lvl]
        e_q = e_q_all[:, sl]
        state_t = state_ref[a]
        o = jnp.dot(scores.astype(BF16), va, preferred_element_type=F32)
        o = o + jnp.sum(q[:, sl] * k[:, sl], axis=-1, keepdims=True) * v[:, sl]
        o = o + lax.dot_general(qa * e_q.astype(BF16), state_t.astype(BF16), _NT,
                                preferred_element_type=F32)
        state_ref[a] = state_t * e_q[total_row:total_row + 1, :] + jnp.dot(
            v[:, sl].T.astype(BF16), ka * e_k_all[:, sl], preferred_element_type=F32)
        outs.append(o)
    return jnp.concatenate(outs, axis=-1)


def _hgrn_body(q_ref, v_ref, zf_ref, zb_ref, g_ref, lb_ref, gain_ref,
               sums_f_ref, masks_f_ref, sums_b_ref, masks_b_ref,
               o_ref, of_ref, ob_ref, state_f_ref, state_b_ref, cum_ref):
    c = masks_f_ref.shape[1]
    n_chunks = q_ref.shape[1] // c
    per_step = cum_ref.shape[0] // 2
    state_f_ref[...] = jnp.zeros_like(state_f_ref)
    state_b_ref[...] = jnp.zeros_like(state_b_ref)

    def finish(r, o):
        g = g_ref[0, pl.ds(r, c), :].astype(F32)
        normed = [
            _rms_norm_rows(o[:, a * HEAD_DIM:(a + 1) * HEAD_DIM],
                           gain_ref[:, a * HEAD_DIM:(a + 1) * HEAD_DIM])
            for a in range(o.shape[1] // HEAD_DIM)
        ]
        y = jnp.concatenate(normed, axis=-1) * (g * jax.nn.sigmoid(g))
        o_ref[0, pl.ds(r, c), :] = y.astype(o_ref.dtype)

    def rows(ref, r):
        return ref[0, pl.ds(r, c), :].astype(F32)

    def scan_step(n, carry, *, other_half_done):
        for u in range(per_step):
            m = n * per_step + u
            rf = pl.multiple_of(m * c, c)
            rb = pl.multiple_of((n_chunks - 1 - m) * c, c)
            o_f = _hgrn_chunk(
                rows(q_ref, rf), rows(v_ref, rf), rows(zf_ref, rf),
                lb_ref[0:1, :], sums_f_ref, masks_f_ref, state_f_ref, cum_ref.at[2 * u], False)
            o_b = _hgrn_chunk(
                rows(q_ref, rb), rows(v_ref, rb), rows(zb_ref, rb),
                lb_ref[1:2, :], sums_b_ref, masks_b_ref, state_b_ref, cum_ref.at[2 * u + 1], True)
            if other_half_done:
                finish(rf, o_f + ob_ref[pl.ds(rf, c), :])
                finish(rb, of_ref[pl.ds(rb, c), :] + o_b)
            else:
                of_ref[pl.ds(rf, c), :] = o_f
                ob_ref[pl.ds(rb, c), :] = o_b
        return carry

    n_steps = n_chunks // per_step
    if n_steps % 2 == 0:
        lax.fori_loop(0, n_steps // 2, functools.partial(scan_step, other_half_done=False), 0)
        lax.fori_loop(n_steps // 2, n_steps, functools.partial(scan_step, other_half_done=True), 0)
    else:
        lax.fori_loop(0, n_steps, functools.partial(scan_step, other_half_done=False), 0)

        def finish_step(n, carry):
            r = pl.multiple_of(n * c, c)
            finish(r, of_ref[pl.ds(r, c), :] + ob_ref[pl.ds(r, c), :])
            return carry

        lax.fori_loop(0, n_chunks, finish_step, 0)


def _hgrn(p3, lb, gain, width):
    b, s, _ = p3.shape
    heads = width // HEAD_DIM
    nh = 2 if heads % 2 == 0 else 1
    wb = nh * HEAD_DIM
    n_blk = width // wb
    c = min(HGRN_CHUNK, s)
    sums_f, masks_f, sums_b, masks_b = _hgrn_constants(c)
    per_step = HGRN_CHUNKS_PER_STEP if (s // c) % HGRN_CHUNKS_PER_STEP == 0 else 1

    def sect(k):
        return pl.BlockSpec((1, s, wb), lambda bi, hi, k=k: (bi, 0, k * n_blk + hi))

    def whole(arr):
        return pl.BlockSpec(arr.shape, lambda bi, hi, nd=arr.ndim: (0,) * nd)

    return pl.pallas_call(
        _hgrn_body,
        out_shape=jax.ShapeDtypeStruct((b, s, width), BF16),
        grid=(b, n_blk),
        in_specs=[
            sect(0), sect(1), sect(2), sect(3), sect(4),
            pl.BlockSpec((2, wb), lambda bi, hi: (0, hi)),
            pl.BlockSpec((1, wb), lambda bi, hi: (0, hi)),
            whole(sums_f), whole(masks_f), whole(sums_b), whole(masks_b),
        ],
        out_specs=pl.BlockSpec((1, s, wb), lambda bi, hi: (bi, 0, hi)),
        scratch_shapes=[
            pltpu.VMEM((s, wb), F32), pltpu.VMEM((s, wb), F32),
            pltpu.VMEM((nh, HEAD_DIM, HEAD_DIM), F32), pltpu.VMEM((nh, HEAD_DIM, HEAD_DIM), F32),
            pltpu.VMEM((2 * per_step, c, wb), F32),
        ],
        compiler_params=_params("parallel", "parallel"),
        name="hgrn",
    )(p3, p3, p3, p3, p3, lb, gain.reshape(1, width), sums_f, masks_f, sums_b, masks_b)


def _attn_body(lam_ref, q_ref, k_ref, v_ref, cos_ref, sin_ref, qg_ref, kg_ref, og_ref,
               o_ref, q1_ref, q2_ref, kb_ref, vb_ref, sc_ref, *, out_scale):
    s = q_ref.shape[1]
    quarter = ATT_MAP_DIM // 2

    def lane_map(shape, axis):
        return (lax.broadcasted_iota(jnp.int32, shape, axis) // quarter) % 2

    first_map = lane_map((1, HEAD_DIM), 1) == 0
    square = (HEAD_DIM, HEAD_DIM)
    same_map = jnp.where(lane_map(square, 0) == lane_map(square, 1), 1.0, 0.0).astype(BF16)
    cos = cos_ref[0]
    sin = sin_ref[0]

    def prep(t, gain):
        sq = t * t
        hi = sq.astype(BF16)
        lo = (sq - hi.astype(F32)).astype(BF16)
        ms = (jnp.dot(hi, same_map, preferred_element_type=F32)
              + jnp.dot(lo, same_map, preferred_element_type=F32)) * (1.0 / ATT_MAP_DIM)
        y = t * lax.rsqrt(ms + EPS) * gain
        return y * cos + pltpu.roll(y, HEAD_DIM // 2, 1) * sin

    qr = prep(q_ref[0].astype(F32), qg_ref[...]) * (ATT_MAP_DIM ** -0.5 * math.log2(math.e))
    q1_ref[...] = jnp.where(first_map, qr, 0.0).astype(BF16)
    q2_ref[...] = jnp.where(first_map, 0.0, qr).astype(BF16)
    kb_ref[...] = prep(k_ref[0].astype(F32), kg_ref[...]).astype(BF16)
    vb_ref[...] = v_ref[0].astype(BF16)
    lam = lam_ref[0]
    tq = min(ATT_Q_BLOCK, s)

    n_blocks = s // tq

    def scores(n, slot):
        r = n * tq
        for m, qm_ref in enumerate((q1_ref, q2_ref)):
            sc_ref[slot, m] = lax.dot_general(qm_ref[pl.ds(r, tq), :], kb_ref[...], _NT,
                                              preferred_element_type=F32)

    def finish(n, slot):
        r = n * tq
        e, inv = [], []
        for m in range(2):
            sc = sc_ref[slot, m]
            em = jnp.exp2(sc - jnp.max(sc, axis=-1, keepdims=True))
            e.append(em.astype(BF16))
            inv.append(1.0 / jnp.sum(em, axis=-1, keepdims=True))
        a = e[0] * inv[0].astype(BF16) - e[1] * (lam * inv[1]).astype(BF16)
        o = jnp.dot(a, vb_ref[...], preferred_element_type=F32)
        y = _rms_norm_rows(o, og_ref[...]) * out_scale
        o_ref[0, pl.ds(r, tq), :] = y.astype(o_ref.dtype)

    def scores_ahead(n, slot):
        scores(n, slot)
        tail = sc_ref[slot, 1, tq - BF16_ROWS:, s - HEAD_DIM:]
        zero = (pltpu.bitcast(tail, jnp.uint32) >> 16) >> 16
        vb_ref[0:BF16_ROWS, :] = vb_ref[0:BF16_ROWS, :] + zero.astype(F32).astype(BF16)

    scores(0, 0)
    for n in range(n_blocks):
        if n + 1 < n_blocks:
            scores_ahead(n + 1, (n + 1) % 2)
        finish(n, n % 2)


def _attn(p3, cos_t, sin_t, lam, q_gain, k_gain, out_gain, col0, width, out_scale):
    b, s, _ = p3.shape
    heads = width // HEAD_DIM
    blk0 = col0 // HEAD_DIM

    def sect(k):
        return pl.BlockSpec((1, s, HEAD_DIM), lambda bi, hi, k=k: (bi, 0, blk0 + k * heads + hi))

    def tiled_gain(g):
        half = ATT_MAP_DIM // 2
        return jnp.concatenate([g[:half], g[:half], g[half:], g[half:]]).reshape(1, HEAD_DIM)

    row = pl.BlockSpec((1, HEAD_DIM), lambda bi, hi: (0, 0))
    table = pl.BlockSpec((1, s, HEAD_DIM), lambda bi, hi: (bi, 0, 0))
    return pl.pallas_call(
        functools.partial(_attn_body, out_scale=out_scale),
        out_shape=jax.ShapeDtypeStruct((b, s, width), BF16),
        grid=(b, heads),
        in_specs=[
            pl.BlockSpec(memory_space=pltpu.SMEM),
            sect(0), sect(1), sect(2), table, table, row, row,
            pl.BlockSpec((1, HEAD_DIM), lambda bi, hi: (0, hi)),
        ],
        out_specs=pl.BlockSpec((1, s, HEAD_DIM), lambda bi, hi: (bi, 0, hi)),
        scratch_shapes=[pltpu.VMEM((s, HEAD_DIM), BF16) for _ in range(4)]
        + [pltpu.VMEM((2, 2, min(ATT_Q_BLOCK, s), s), F32)],
        compiler_params=_params("parallel", "parallel"),
        name="attn",
    )(lam.reshape(1), p3, p3, p3, cos_t, sin_t, tiled_gain(q_gain), tiled_gain(k_gain),
      out_gain.reshape(1, width))


def _conv_body(b_ref, c_ref, u_ref, w_ref, bias_ref, gain_ref, o_ref):
    s = u_ref.shape[1]
    v = c_ref[0].astype(F32) * u_ref[0].astype(F32)
    row = lax.broadcasted_iota(jnp.int32, (s, 1), 0)
    prev = jnp.where(row == 0, 0.0, pltpu.roll(v, 1, 0))
    nxt = jnp.where(row == s - 1, 0.0, pltpu.roll(v, s - 1, 0))
    y = bias_ref[...] + prev * w_ref[0:1, :]
    y = y + v * w_ref[1:2, :]
    y = y + nxt * w_ref[2:3, :]
    y = b_ref[0].astype(F32) * y
    for grp in range(y.shape[1] // HEAD_DIM):
        lanes = slice(grp * HEAD_DIM, (grp + 1) * HEAD_DIM)
        o_ref[0, :, lanes] = _rms_norm_rows(y[:, lanes], gain_ref[:, lanes]).astype(o_ref.dtype)


def _conv(p3, w, bias, gain, col0, width):
    b, s, _ = p3.shape
    assert col0 % width == 0, (col0, width)
    blk0 = col0 // width

    def sect(k):
        return pl.BlockSpec((1, s, width), lambda bi, k=k: (bi, 0, blk0 + k))

    def whole(n_rows):
        return pl.BlockSpec((n_rows, width), lambda bi: (0, 0))

    return pl.pallas_call(
        _conv_body,
        out_shape=jax.ShapeDtypeStruct((b, s, width), BF16),
        grid=(b,),
        in_specs=[sect(0), sect(1), sect(2), whole(w.shape[0]), whole(1), whole(1)],
        out_specs=pl.BlockSpec((1, s, width), lambda bi: (bi, 0, 0)),
        compiler_params=_params("parallel"),
        name="conv",
    )(p3, p3, p3, w, bias.reshape(1, width), gain.reshape(1, width))


def _proj_out_body(x_ref, ya_ref, yb_ref, yc_ref, wa_ref, wb_ref, wc_ref, o_ref):
    acc = jnp.dot(ya_ref[...], wa_ref[...], preferred_element_type=F32)
    acc = acc + jnp.dot(yb_ref[...], wb_ref[...], preferred_element_type=F32)
    acc = acc + jnp.dot(yc_ref[...], wc_ref[...], preferred_element_type=F32)
    o_ref[...] = x_ref[...] + acc


def _proj_out(x2, ya, yb, yc, w):
    t, d = x2.shape
    wa, wb, wc = ya.shape[1], yb.shape[1], yc.shape[1]
    assert wa % wb == 0 and (wa + wb) % wc == 0, (wa, wb, wc)
    tm = _tile(t, PROJ_OUT_TOKEN_TILE, SUBLANES)

    def rows(width):
        return pl.BlockSpec((tm, width), lambda i: (i, 0))

    def w_rows(n_rows, row0):
        return pl.BlockSpec((n_rows, d), lambda i, blk=row0 // n_rows: (blk, 0))

    return pl.pallas_call(
        _proj_out_body,
        out_shape=jax.ShapeDtypeStruct((t, d), F32),
        grid=(t // tm,),
        in_specs=[rows(d), rows(wa), rows(wb), rows(wc),
                  w_rows(wa, 0), w_rows(wb, wa), w_rows(wc, wa + wb)],
        out_specs=rows(d),
        compiler_params=_params("parallel"),
        name="proj_out",
    )(x2, ya, yb, yc, w, w, w)


def _cast_cols_body(w_ref, o_ref):
    f = w_ref.shape[1]
    o_ref[:, :f] = w_ref[...].astype(o_ref.dtype)
    if o_ref.shape[1] > f:
        o_ref[:, f:] = jnp.zeros((o_ref.shape[0], o_ref.shape[1] - f), o_ref.dtype)


def _cast_pad_cols(w, layer, cols):
    _, r, f = w.shape
    tr = _tile(r, CAST_ROW_TILE, SUBLANES)
    return pl.pallas_call(
        _cast_cols_body,
        out_shape=jax.ShapeDtypeStruct((r, cols), BF16),
        grid=(r // tr,),
        in_specs=[pl.BlockSpec((None, tr, f), lambda i: (layer, i, 0))],
        out_specs=pl.BlockSpec((tr, cols), lambda i: (i, 0)),
        compiler_params=_params("parallel"),
        name="cast_pad_cols",
    )(w)


def _cast_rows_body(w_ref, o_ref, *, valid_rows):
    tr = o_ref.shape[0]
    row = pl.program_id(0) * tr + lax.broadcasted_iota(jnp.int32, (tr, 1), 0)
    o_ref[...] = jnp.where(row < valid_rows, w_ref[...], 0.0).astype(o_ref.dtype)


def _cast_pad_rows(w, layer, rows):
    _, f, c = w.shape
    tr = _tile(rows, FFN_TILE, SUBLANES)
    assert (rows - f) < tr, (rows, f, tr)
    return pl.pallas_call(
        functools.partial(_cast_rows_body, valid_rows=f),
        out_shape=jax.ShapeDtypeStruct((rows, c), BF16),
        grid=(rows // tr,),
        in_specs=[pl.BlockSpec((None, tr, c), lambda i: (layer, i, 0))],
        out_specs=pl.BlockSpec((tr, c), lambda i: (i, 0)),
        compiler_params=_params("parallel"),
        name="cast_pad_rows",
    )(w)


def _cast_body(w_ref, o_ref, *, reorder_tile):
    @pl.when(pl.program_id(0) != reorder_tile)
    def _():
        o_ref[...] = w_ref[...].astype(o_ref.dtype)

    @pl.when(pl.program_id(0) == reorder_tile)
    def _():
        quarter = lax.broadcasted_iota(jnp.int32, (1, HEAD_DIM), 1) // (ATT_MAP_DIM // 2)
        for blk in range(w_ref.shape[1] // HEAD_DIM):
            lanes = slice(blk * HEAD_DIM, (blk + 1) * HEAD_DIM)
            t = w_ref[:, lanes]
            t = jnp.where(quarter == 1, pltpu.roll(t, HEAD_DIM - ATT_MAP_DIM // 2, 1),
                          jnp.where(quarter == 2, pltpu.roll(t, ATT_MAP_DIM // 2, 1), t))
            o_ref[:, lanes] = t.astype(o_ref.dtype)


def _cast(w, layer, reorder_cols=None):
    _, r, n = w.shape
    if reorder_cols is None:
        tn, reorder_tile = _tile(n, CAST_COL_TILE, LANES), -1
    else:
        start, tn = reorder_cols
        assert start % tn == 0 and n % tn == 0, (start, tn, n)
        reorder_tile = start // tn
    return pl.pallas_call(
        functools.partial(_cast_body, reorder_tile=reorder_tile),
        out_shape=jax.ShapeDtypeStruct((r, n), BF16),
        grid=(n // tn,),
        in_specs=[pl.BlockSpec((None, r, tn), lambda j: (layer, 0, j))],
        out_specs=pl.BlockSpec((r, tn), lambda j: (0, j)),
        compiler_params=_params("parallel"),
        name="cast",
    )(w)


def _ffn_weights(wg, wu, wd, layer):
    f = wg.shape[-1]
    fp = f + ((-f) % FFN_TILE if f > FFN_TILE else 0)
    return _cast_pad_cols(wg, layer, fp), _cast_pad_cols(wu, layer, fp), _cast_pad_rows(wd, layer, fp)


def kernel(x, positions, ffn1_norm, ffn1_w_gate, ffn1_w_up, ffn1_w_down, mix_norm, w_in, hgrn_lb_logits, hgrn_norm, da_q_norm, da_k_norm, da_lambda_q1, da_lambda_k1, da_lambda_q2, da_lambda_k2, da_out_norm, conv_w, conv_b, conv_norm, w_out, ffn2_norm, ffn2_w_gate, ffn2_w_up, ffn2_w_down):
    b, s, d = x.shape
    depth = w_in.shape[0]
    hg_width = d // 2
    da_width = d // 4
    sc_width = d - hg_width - da_width
    da_col0 = 5 * hg_width
    sc_col0 = da_col0 + 3 * da_width

    half = ATT_MAP_DIM // 2
    inv_freq = ROPE_THETA ** (-jnp.arange(0, ATT_MAP_DIM, 2, dtype=F32) / ATT_MAP_DIM)
    ang = positions.astype(F32)[..., None] * inv_freq
    cos_t = jnp.tile(jnp.cos(ang), (1, 1, HEAD_DIM // half))
    sin_half = jnp.sin(ang)
    sin_t = jnp.concatenate([-sin_half, -sin_half, sin_half, sin_half], axis=-1)

    lb_all = jnp.cumsum(jax.nn.softmax(hgrn_lb_logits.astype(F32), axis=1), axis=1)
    lb_all = lb_all - lb_all[:, :1]

    x2 = x.reshape(b * s, d)
    for layer in range(depth):
        x2 = _ffn(x2, ffn1_norm[layer], *_ffn_weights(ffn1_w_gate, ffn1_w_up, ffn1_w_down, layer))

        w_in_l = _cast(w_in, layer, reorder_cols=(da_col0, 2 * da_width))
        p3 = _proj_in(x2, mix_norm[layer], w_in_l).reshape(b, s, -1)

        y_a = _hgrn(p3, lb_all[:, layer], hgrn_norm[layer], hg_width)

        lam_init = 0.8 - 0.6 * math.exp(-0.3 * layer)
        lam = (jnp.exp(jnp.sum(da_lambda_q1[layer].astype(F32) * da_lambda_k1[layer].astype(F32)))
               - jnp.exp(jnp.sum(da_lambda_q2[layer].astype(F32) * da_lambda_k2[layer].astype(F32)))
               + lam_init)
        y_b = _attn(p3, cos_t, sin_t, lam, da_q_norm[layer], da_k_norm[layer], da_out_norm[layer],
                    da_col0, da_width, 1.0 - lam_init)

        y_c = _conv(p3, conv_w[layer], conv_b[layer], conv_norm[layer], sc_col0, sc_width)

        x2 = _proj_out(x2, y_a.reshape(b * s, -1), y_b.reshape(b * s, -1), y_c.reshape(b * s, -1),
                       _cast(w_out, layer))

        x2 = _ffn(x2, ffn2_norm[layer], *_ffn_weights(ffn2_w_gate, ffn2_w_up, ffn2_w_down, layer))
    return x2.reshape(b, s, d)
```

```python
import functools
import math

import jax
import jax.numpy as jnp
import numpy as np
from jax import lax
from jax.experimental import pallas as pl
from jax.experimental.pallas import tpu as pltpu

F32 = jnp.float32
BF16 = jnp.bfloat16

EPS = 1e-6
ROPE_THETA = 10000.0
LANES = 128
SUBLANES = 8
BF16_ROWS = 2 * SUBLANES
HEAD_DIM = LANES
ATT_MAP_DIM = 64
HGRN_CHUNK = 128
HGRN_CHUNKS_PER_STEP = 4
ATT_Q_BLOCK = 512
FFN_TILE = 512
FFN_TOKEN_TILE = 1024
PROJ_IN_TOKEN_TILE = 1024
PROJ_IN_COL_TILE = 2048
PROJ_OUT_TOKEN_TILE = 512
CAST_ROW_TILE = 256
CAST_COL_TILE = 1024
VMEM_LIMIT_BYTES = 56 * 1024 * 1024

_NT = (((1,), (1,)), ((), ()))


def _rms_norm_rows(x, gain):
    ms = jnp.mean(x * x, axis=-1, keepdims=True)
    return x * lax.rsqrt(ms + EPS) * gain


def _tile(n, pref, quantum):
    if n <= pref:
        return n
    t = (pref // quantum) * quantum
    while n % t:
        t -= quantum
    return t


def _params(*semantics):
    return pltpu.CompilerParams(dimension_semantics=semantics, vmem_limit_bytes=VMEM_LIMIT_BYTES)


def _ffn_body(x_ref, gain_ref, wg_ref, wu_ref, wd_ref, o_ref, h_ref):
    j = pl.program_id(1)

    @pl.when(j == 0)
    def _():
        x = x_ref[...]
        h_ref[...] = _rms_norm_rows(x, gain_ref[...]).astype(BF16)
        o_ref[...] = x

    h = h_ref[...]
    g = jnp.dot(h, wg_ref[...], preferred_element_type=F32)
    u = jnp.dot(h, wu_ref[...], preferred_element_type=F32)
    a = (g * jax.nn.sigmoid(g) * (0.5 * u)).astype(BF16)
    o_ref[...] += jnp.dot(a, wd_ref[...], preferred_element_type=F32)


def _ffn(x2, gain, wg, wu, wd):
    t, d = x2.shape
    fp = wg.shape[1]
    tm = _tile(t, FFN_TOKEN_TILE, SUBLANES)
    tf = _tile(fp, FFN_TILE, LANES)
    return pl.pallas_call(
        _ffn_body,
        out_shape=jax.ShapeDtypeStruct((t, d), F32),
        grid=(t // tm, fp // tf),
        in_specs=[
            pl.BlockSpec((tm, d), lambda i, j: (i, 0)),
            pl.BlockSpec((1, d), lambda i, j: (0, 0)),
            pl.BlockSpec((d, tf), lambda i, j: (0, j)),
            pl.BlockSpec((d, tf), lambda i, j: (0, j)),
            pl.BlockSpec((tf, d), lambda i, j: (j, 0)),
        ],
        out_specs=pl.BlockSpec((tm, d), lambda i, j: (i, 0)),
        scratch_shapes=[pltpu.VMEM((tm, d), BF16)],
        compiler_params=_params("parallel", "arbitrary"),
        name="ffn",
    )(x2, gain.reshape(1, d), wg, wu, wd)


def _proj_in_body(x_ref, gain_ref, w_ref, o_ref, h_ref):
    @pl.when(pl.program_id(1) == 0)
    def _():
        h_ref[...] = _rms_norm_rows(x_ref[...], gain_ref[...]).astype(BF16)

    o_ref[...] = jnp.dot(h_ref[...], w_ref[...], preferred_element_type=F32)


def _proj_in(x2, gain, w):
    t, d = x2.shape
    n = w.shape[1]
    tm = _tile(t, PROJ_IN_TOKEN_TILE, SUBLANES)
    tn = _tile(n, PROJ_IN_COL_TILE, LANES)
    return pl.pallas_call(
        _proj_in_body,
        out_shape=jax.ShapeDtypeStruct((t, n), F32),
        grid=(t // tm, n // tn),
        in_specs=[
            pl.BlockSpec((tm, d), lambda i, j: (i, 0)),
            pl.BlockSpec((1, d), lambda i, j: (0, 0)),
            pl.BlockSpec((d, tn), lambda i, j: (0, j)),
        ],
        out_specs=pl.BlockSpec((tm, tn), lambda i, j: (i, j)),
        scratch_shapes=[pltpu.VMEM((tm, d), BF16)],
        compiler_params=_params("parallel", "arbitrary"),
        name="proj_in",
    )(x2, gain.reshape(1, d), w)


def _hgrn_constants(c):
    t = np.arange(c)[:, None]
    r = np.arange(c)[None, :]
    masks, small = [], []
    h = c // 2
    while h >= 1:
        start = (t // h) * h
        upper = (t // h) % 2 == 1
        masks.append((t // (2 * h) == r // (2 * h)) & upper & ((r // h) % 2 == 0))
        if h <= 2:
            small.append(np.where(upper, (r >= start) & (r <= t), (r > t) & (r <= start + h - 1)))
        h //= 2
    masks = np.stack(masks).astype(np.float32)
    sums = np.stack([r <= t] + small).astype(np.float32)
    out = []
    for flip in (False, True):
        m = masks[:, ::-1, ::-1] if flip else masks
        s = (sums[:, ::-1, ::-1] if flip else sums).reshape(-1, c)
        out.append(jnp.asarray(np.concatenate([s, s], axis=1), dtype=BF16))
        out.append(jnp.asarray(m, dtype=F32))
    return out


def _neg_abs(d):
    sign = jnp.int32(-2 ** 31)
    return lax.bitcast_convert_type(lax.bitcast_convert_type(d, jnp.int32) | sign, F32)


def _hgrn_level_exponents(cum, cum_ref, reverse):
    c = cum.shape[0]
    xs = []
    h = c // 2
    while h >= 4:
        pieces = []
        for j in range(c // (2 * h)):
            r0 = j * 2 * h
            b = cum_ref[pl.ds(r0 + h if reverse else r0 + h - 1, 1), :]
            if h >= 8:
                lo_rows, hi_rows = cum[r0:r0 + h], cum[r0 + h:r0 + 2 * h]
                pieces.extend((lo_rows - b, b - hi_rows) if reverse else (b - lo_rows, hi_rows - b))
            else:
                pieces.append(_neg_abs(cum[r0:r0 + 2 * h] - b))
        xs.append(jnp.concatenate(pieces, axis=0))
        h //= 2
    return xs


def _hgrn_chunk(q, v, z, lb, sums_ref, masks_ref, state_ref, cum_ref, reverse):
    c = q.shape[0]
    f = lb + (1.0 - lb) * jax.nn.sigmoid(z)
    k = 1.0 - f
    lf = jnp.log2(f)
    hi = lf.astype(BF16)
    lo = (lf - hi.astype(F32)).astype(BF16)
    sums = jnp.dot(sums_ref[...], jnp.concatenate([hi, lo], axis=0), preferred_element_type=F32)
    cum = sums[:c]
    cum_ref[...] = cum
    total_row = 0 if reverse else c - 1
    x_key = cum_ref[pl.ds(total_row, 1), :] - cum
    x_levels = _hgrn_level_exponents(cum, cum_ref, reverse) + [sums[c:2 * c], sums[2 * c:]]
    e_levels = [jnp.exp2(x).astype(BF16) for x in x_levels]
    e_q_all = jnp.exp2(cum)
    e_k_all = jnp.exp2(x_key).astype(BF16)
    q16, k16, v16 = q.astype(BF16), k.astype(BF16), v.astype(BF16)
    outs = []
    for a in range(q.shape[1] // HEAD_DIM):
        sl = slice(a * HEAD_DIM, (a + 1) * HEAD_DIM)
        qa, ka, va = q16[:, sl], k16[:, sl], v16[:, sl]
        ka_t = ka.T
        scores = jnp.zeros((c, c), F32)
        for lvl, e_l in enumerate(e_levels):
            el = e_l[:, sl]
            s_l = jnp.dot(qa * el, ka_t * el.T, preferred_element_type=F32)
            scores = scores + s_l * masks_ref[lvl]
        e_q = e_q_all[:, sl]
        state_t = state_ref[a]
        o = jnp.dot(scores.astype(BF16), va, preferred_element_type=F32)
        o = o + jnp.sum(q[:, sl] * k[:, sl], axis=-1, keepdims=True) * v[:, sl]
        o = o + lax.dot_general(qa * e_q.astype(BF16), state_t.astype(BF16), _NT,
                                preferred_element_type=F32)
        state_ref[a] = state_t * e_q[total_row:total_row + 1, :] + jnp.dot(
            v[:, sl].T.astype(BF16), ka * e_k_all[:, sl], preferred_element_type=F32)
        outs.append(o)
    return jnp.concatenate(outs, axis=-1)


def _hgrn_body(q_ref, v_ref, zf_ref, zb_ref, g_ref, lb_ref, gain_ref,
               sums_f_ref, masks_f_ref, sums_b_ref, masks_b_ref,
               o_ref, of_ref, ob_ref, state_f_ref, state_b_ref, cum_ref):
    c = masks_f_ref.shape[1]
    n_chunks = q_ref.shape[1] // c
    per_step = cum_ref.shape[0] // 2
    state_f_ref[...] = jnp.zeros_like(state_f_ref)
    state_b_ref[...] = jnp.zeros_like(state_b_ref)

    def finish(r, o):
        g = g_ref[0, pl.ds(r, c), :]
        normed = [
            _rms_norm_rows(o[:, a * HEAD_DIM:(a + 1) * HEAD_DIM],
                           gain_ref[:, a * HEAD_DIM:(a + 1) * HEAD_DIM])
            for a in range(o.shape[1] // HEAD_DIM)
        ]
        y = jnp.concatenate(normed, axis=-1) * (g * jax.nn.sigmoid(g))
        o_ref[0, pl.ds(r, c), :] = y.astype(o_ref.dtype)

    def scan_step(n, carry, *, other_half_done):
        for u in range(per_step):
            m = n * per_step + u
            rf = pl.multiple_of(m * c, c)
            rb = pl.multiple_of((n_chunks - 1 - m) * c, c)
            o_f = _hgrn_chunk(
                q_ref[0, pl.ds(rf, c), :], v_ref[0, pl.ds(rf, c), :], zf_ref[0, pl.ds(rf, c), :],
                lb_ref[0:1, :], sums_f_ref, masks_f_ref, state_f_ref, cum_ref.at[2 * u], False)
            o_b = _hgrn_chunk(
                q_ref[0, pl.ds(rb, c), :], v_ref[0, pl.ds(rb, c), :], zb_ref[0, pl.ds(rb, c), :],
                lb_ref[1:2, :], sums_b_ref, masks_b_ref, state_b_ref, cum_ref.at[2 * u + 1], True)
            if other_half_done:
                finish(rf, o_f + ob_ref[pl.ds(rf, c), :])
                finish(rb, of_ref[pl.ds(rb, c), :] + o_b)
            else:
                of_ref[pl.ds(rf, c), :] = o_f
                ob_ref[pl.ds(rb, c), :] = o_b
        return carry

    n_steps = n_chunks // per_step
    if n_steps % 2 == 0:
        lax.fori_loop(0, n_steps // 2, functools.partial(scan_step, other_half_done=False), 0)
        lax.fori_loop(n_steps // 2, n_steps, functools.partial(scan_step, other_half_done=True), 0)
    else:
        lax.fori_loop(0, n_steps, functools.partial(scan_step, other_half_done=False), 0)

        def finish_step(n, carry):
            r = pl.multiple_of(n * c, c)
            finish(r, of_ref[pl.ds(r, c), :] + ob_ref[pl.ds(r, c), :])
            return carry

        lax.fori_loop(0, n_chunks, finish_step, 0)


def _hgrn(p3, lb, gain, width):
    b, s, _ = p3.shape
    heads = width // HEAD_DIM
    nh = 2 if heads % 2 == 0 else 1
    wb = nh * HEAD_DIM
    n_blk = width // wb
    c = min(HGRN_CHUNK, s)
    sums_f, masks_f, sums_b, masks_b = _hgrn_constants(c)
    per_step = HGRN_CHUNKS_PER_STEP if (s // c) % HGRN_CHUNKS_PER_STEP == 0 else 1

    def sect(k):
        return pl.BlockSpec((1, s, wb), lambda bi, hi, k=k: (bi, 0, k * n_blk + hi))

    def whole(arr):
        return pl.BlockSpec(arr.shape, lambda bi, hi, nd=arr.ndim: (0,) * nd)

    return pl.pallas_call(
        _hgrn_body,
        out_shape=jax.ShapeDtypeStruct((b, s, width), BF16),
        grid=(b, n_blk),
        in_specs=[
            sect(0), sect(1), sect(2), sect(3), sect(4),
            pl.BlockSpec((2, wb), lambda bi, hi: (0, hi)),
            pl.BlockSpec((1, wb), lambda bi, hi: (0, hi)),
            whole(sums_f), whole(masks_f), whole(sums_b), whole(masks_b),
        ],
        out_specs=pl.BlockSpec((1, s, wb), lambda bi, hi: (bi, 0, hi)),
        scratch_shapes=[
            pltpu.VMEM((s, wb), F32), pltpu.VMEM((s, wb), F32),
            pltpu.VMEM((nh, HEAD_DIM, HEAD_DIM), F32), pltpu.VMEM((nh, HEAD_DIM, HEAD_DIM), F32),
            pltpu.VMEM((2 * per_step, c, wb), F32),
        ],
        compiler_params=_params("parallel", "parallel"),
        name="hgrn",
    )(p3, p3, p3, p3, p3, lb, gain.reshape(1, width), sums_f, masks_f, sums_b, masks_b)


def _attn_body(lam_ref, q_ref, k_ref, v_ref, cos_ref, sin_ref, qg_ref, kg_ref, og_ref,
               o_ref, q1_ref, q2_ref, kb_ref, vb_ref, sc_ref, *, out_scale):
    s = q_ref.shape[1]
    quarter = ATT_MAP_DIM // 2

    def lane_map(shape, axis):
        return (lax.broadcasted_iota(jnp.int32, shape, axis) // quarter) % 2

    first_map = lane_map((1, HEAD_DIM), 1) == 0
    square = (HEAD_DIM, HEAD_DIM)
    same_map = jnp.where(lane_map(square, 0) == lane_map(square, 1), 1.0, 0.0).astype(BF16)
    cos = cos_ref[0]
    sin = sin_ref[0]

    def prep(t, gain):
        sq = t * t
        hi = sq.astype(BF16)
        lo = (sq - hi.astype(F32)).astype(BF16)
        ms = (jnp.dot(hi, same_map, preferred_element_type=F32)
              + jnp.dot(lo, same_map, preferred_element_type=F32)) * (1.0 / ATT_MAP_DIM)
        y = t * lax.rsqrt(ms + EPS) * gain
        return y * cos + pltpu.roll(y, HEAD_DIM // 2, 1) * sin

    qr = prep(q_ref[0], qg_ref[...]) * (ATT_MAP_DIM ** -0.5 * math.log2(math.e))
    q1_ref[...] = jnp.where(first_map, qr, 0.0).astype(BF16)
    q2_ref[...] = jnp.where(first_map, 0.0, qr).astype(BF16)
    kb_ref[...] = prep(k_ref[0], kg_ref[...]).astype(BF16)
    vb_ref[...] = v_ref[0].astype(BF16)
    lam = lam_ref[0]
    tq = min(ATT_Q_BLOCK, s)

    n_blocks = s // tq

    def scores(n, slot):
        r = n * tq
        for m, qm_ref in enumerate((q1_ref, q2_ref)):
            sc_ref[slot, m] = lax.dot_general(qm_ref[pl.ds(r, tq), :], kb_ref[...], _NT,
                                              preferred_element_type=F32)

    def finish(n, slot):
        r = n * tq
        e, inv = [], []
        for m in range(2):
            sc = sc_ref[slot, m]
            em = jnp.exp2(sc - jnp.max(sc, axis=-1, keepdims=True))
            e.append(em.astype(BF16))
            inv.append(1.0 / jnp.sum(em, axis=-1, keepdims=True))
        a = e[0] * inv[0].astype(BF16) - e[1] * (lam * inv[1]).astype(BF16)
        o = jnp.dot(a, vb_ref[...], preferred_element_type=F32)
        y = _rms_norm_rows(o, og_ref[...]) * out_scale
        o_ref[0, pl.ds(r, tq), :] = y.astype(o_ref.dtype)

    scores(0, 0)
    for n in range(n_blocks):
        if n + 1 < n_blocks:
            scores(n + 1, (n + 1) % 2)
        finish(n, n % 2)


def _attn(p3, cos_t, sin_t, lam, q_gain, k_gain, out_gain, col0, width, out_scale):
    b, s, _ = p3.shape
    heads = width // HEAD_DIM
    blk0 = col0 // HEAD_DIM

    def sect(k):
        return pl.BlockSpec((1, s, HEAD_DIM), lambda bi, hi, k=k: (bi, 0, blk0 + k * heads + hi))

    def tiled_gain(g):
        half = ATT_MAP_DIM // 2
        return jnp.concatenate([g[:half], g[:half], g[half:], g[half:]]).reshape(1, HEAD_DIM)

    row = pl.BlockSpec((1, HEAD_DIM), lambda bi, hi: (0, 0))
    table = pl.BlockSpec((1, s, HEAD_DIM), lambda bi, hi: (bi, 0, 0))
    return pl.pallas_call(
        functools.partial(_attn_body, out_scale=out_scale),
        out_shape=jax.ShapeDtypeStruct((b, s, width), BF16),
        grid=(b, heads),
        in_specs=[
            pl.BlockSpec(memory_space=pltpu.SMEM),
            sect(0), sect(1), sect(2), table, table, row, row,
            pl.BlockSpec((1, HEAD_DIM), lambda bi, hi: (0, hi)),
        ],
        out_specs=pl.BlockSpec((1, s, HEAD_DIM), lambda bi, hi: (bi, 0, hi)),
        scratch_shapes=[pltpu.VMEM((s, HEAD_DIM), BF16) for _ in range(4)]
        + [pltpu.VMEM((2, 2, min(ATT_Q_BLOCK, s), s), F32)],
        compiler_params=_params("parallel", "parallel"),
        name="attn",
    )(lam.reshape(1), p3, p3, p3, cos_t, sin_t, tiled_gain(q_gain), tiled_gain(k_gain),
      out_gain.reshape(1, width))


def _conv_body(b_ref, c_ref, u_ref, w_ref, bias_ref, gain_ref, o_ref):
    s = u_ref.shape[1]
    v = c_ref[0] * u_ref[0]
    row = lax.broadcasted_iota(jnp.int32, (s, 1), 0)
    prev = jnp.where(row == 0, 0.0, pltpu.roll(v, 1, 0))
    nxt = jnp.where(row == s - 1, 0.0, pltpu.roll(v, s - 1, 0))
    y = bias_ref[...] + prev * w_ref[0:1, :]
    y = y + v * w_ref[1:2, :]
    y = y + nxt * w_ref[2:3, :]
    y = b_ref[0] * y
    for grp in range(y.shape[1] // HEAD_DIM):
        lanes = slice(grp * HEAD_DIM, (grp + 1) * HEAD_DIM)
        o_ref[0, :, lanes] = _rms_norm_rows(y[:, lanes], gain_ref[:, lanes]).astype(o_ref.dtype)


def _conv(p3, w, bias, gain, col0, width):
    b, s, _ = p3.shape
    assert col0 % width == 0, (col0, width)
    blk0 = col0 // width

    def sect(k):
        return pl.BlockSpec((1, s, width), lambda bi, k=k: (bi, 0, blk0 + k))

    def whole(n_rows):
        return pl.BlockSpec((n_rows, width), lambda bi: (0, 0))

    return pl.pallas_call(
        _conv_body,
        out_shape=jax.ShapeDtypeStruct((b, s, width), BF16),
        grid=(b,),
        in_specs=[sect(0), sect(1), sect(2), whole(w.shape[0]), whole(1), whole(1)],
        out_specs=pl.BlockSpec((1, s, width), lambda bi: (bi, 0, 0)),
        compiler_params=_params("parallel"),
        name="conv",
    )(p3, p3, p3, w, bias.reshape(1, width), gain.reshape(1, width))


def _proj_out_body(x_ref, ya_ref, yb_ref, yc_ref, wa_ref, wb_ref, wc_ref, o_ref):
    acc = jnp.dot(ya_ref[...], wa_ref[...], preferred_element_type=F32)
    acc = acc + jnp.dot(yb_ref[...], wb_ref[...], preferred_element_type=F32)
    acc = acc + jnp.dot(yc_ref[...], wc_ref[...], preferred_element_type=F32)
    o_ref[...] = x_ref[...] + acc


def _proj_out(x2, ya, yb, yc, w):
    t, d = x2.shape
    wa, wb, wc = ya.shape[1], yb.shape[1], yc.shape[1]
    assert wa % wb == 0 and (wa + wb) % wc == 0, (wa, wb, wc)
    tm = _tile(t, PROJ_OUT_TOKEN_TILE, SUBLANES)

    def rows(width):
        return pl.BlockSpec((tm, width), lambda i: (i, 0))

    def w_rows(n_rows, row0):
        return pl.BlockSpec((n_rows, d), lambda i, blk=row0 // n_rows: (blk, 0))

    return pl.pallas_call(
        _proj_out_body,
        out_shape=jax.ShapeDtypeStruct((t, d), F32),
        grid=(t // tm,),
        in_specs=[rows(d), rows(wa), rows(wb), rows(wc),
                  w_rows(wa, 0), w_rows(wb, wa), w_rows(wc, wa + wb)],
        out_specs=rows(d),
        compiler_params=_params("parallel"),
        name="proj_out",
    )(x2, ya, yb, yc, w, w, w)


def _cast_cols_body(w_ref, o_ref):
    f = w_ref.shape[1]
    o_ref[:, :f] = w_ref[...].astype(o_ref.dtype)
    if o_ref.shape[1] > f:
        o_ref[:, f:] = jnp.zeros((o_ref.shape[0], o_ref.shape[1] - f), o_ref.dtype)


def _cast_pad_cols(w, layer, cols):
    _, r, f = w.shape
    tr = _tile(r, CAST_ROW_TILE, SUBLANES)
    return pl.pallas_call(
        _cast_cols_body,
        out_shape=jax.ShapeDtypeStruct((r, cols), BF16),
        grid=(r // tr,),
        in_specs=[pl.BlockSpec((None, tr, f), lambda i: (layer, i, 0))],
        out_specs=pl.BlockSpec((tr, cols), lambda i: (i, 0)),
        compiler_params=_params("parallel"),
        name="cast_pad_cols",
    )(w)


def _cast_rows_body(w_ref, o_ref, *, valid_rows):
    tr = o_ref.shape[0]
    row = pl.program_id(0) * tr + lax.broadcasted_iota(jnp.int32, (tr, 1), 0)
    o_ref[...] = jnp.where(row < valid_rows, w_ref[...], 0.0).astype(o_ref.dtype)


def _cast_pad_rows(w, layer, rows):
    _, f, c = w.shape
    tr = _tile(rows, FFN_TILE, SUBLANES)
    assert (rows - f) < tr, (rows, f, tr)
    return pl.pallas_call(
        functools.partial(_cast_rows_body, valid_rows=f),
        out_shape=jax.ShapeDtypeStruct((rows, c), BF16),
        grid=(rows // tr,),
        in_specs=[pl.BlockSpec((None, tr, c), lambda i: (layer, i, 0))],
        out_specs=pl.BlockSpec((tr, c), lambda i: (i, 0)),
        compiler_params=_params("parallel"),
        name="cast_pad_rows",
    )(w)


def _cast_body(w_ref, o_ref, *, reorder_tile):
    @pl.when(pl.program_id(0) != reorder_tile)
    def _():
        o_ref[...] = w_ref[...].astype(o_ref.dtype)

    @pl.when(pl.program_id(0) == reorder_tile)
    def _():
        quarter = lax.broadcasted_iota(jnp.int32, (1, HEAD_DIM), 1) // (ATT_MAP_DIM // 2)
        for blk in range(w_ref.shape[1] // HEAD_DIM):
            lanes = slice(blk * HEAD_DIM, (blk + 1) * HEAD_DIM)
            t = w_ref[:, lanes]
            t = jnp.where(quarter == 1, pltpu.roll(t, HEAD_DIM - ATT_MAP_DIM // 2, 1),
                          jnp.where(quarter == 2, pltpu.roll(t, ATT_MAP_DIM // 2, 1), t))
            o_ref[:, lanes] = t.astype(o_ref.dtype)


def _cast(w, layer, reorder_cols=None):
    _, r, n = w.shape
    if reorder_cols is None:
        tn, reorder_tile = _tile(n, CAST_COL_TILE, LANES), -1
    else:
        start, tn = reorder_cols
        assert start % tn == 0 and n % tn == 0, (start, tn, n)
        reorder_tile = start // tn
    return pl.pallas_call(
        functools.partial(_cast_body, reorder_tile=reorder_tile),
        out_shape=jax.ShapeDtypeStruct((r, n), BF16),
        grid=(n // tn,),
        in_specs=[pl.BlockSpec((None, r, tn), lambda j: (layer, 0, j))],
        out_specs=pl.BlockSpec((r, tn), lambda j: (0, j)),
        compiler_params=_params("parallel"),
        name="cast",
    )(w)


def _ffn_weights(wg, wu, wd, layer):
    f = wg.shape[-1]
    fp = f + ((-f) % FFN_TILE if f > FFN_TILE else 0)
    return _cast_pad_cols(wg, layer, fp), _cast_pad_cols(wu, layer, fp), _cast_pad_rows(wd, layer, fp)


def kernel(x, positions, ffn1_norm, ffn1_w_gate, ffn1_w_up, ffn1_w_down, mix_norm, w_in, hgrn_lb_logits, hgrn_norm, da_q_norm, da_k_norm, da_lambda_q1, da_lambda_k1, da_lambda_q2, da_lambda_k2, da_out_norm, conv_w, conv_b, conv_norm, w_out, ffn2_norm, ffn2_w_gate, ffn2_w_up, ffn2_w_down):
    b, s, d = x.shape
    depth = w_in.shape[0]
    hg_width = d // 2
    da_width = d // 4
    sc_width = d - hg_width - da_width
    da_col0 = 5 * hg_width
    sc_col0 = da_col0 + 3 * da_width

    half = ATT_MAP_DIM // 2
    inv_freq = ROPE_THETA ** (-jnp.arange(0, ATT_MAP_DIM, 2, dtype=F32) / ATT_MAP_DIM)
    ang = positions.astype(F32)[..., None] * inv_freq
    cos_t = jnp.tile(jnp.cos(ang), (1, 1, HEAD_DIM // half))
    sin_half = jnp.sin(ang)
    sin_t = jnp.concatenate([-sin_half, -sin_half, sin_half, sin_half], axis=-1)

    lb_all = jnp.cumsum(jax.nn.softmax(hgrn_lb_logits.astype(F32), axis=1), axis=1)
    lb_all = lb_all - lb_all[:, :1]

    x2 = x.reshape(b * s, d)
    for layer in range(depth):
        x2 = _ffn(x2, ffn1_norm[layer], *_ffn_weights(ffn1_w_gate, ffn1_w_up, ffn1_w_down, layer))

        w_in_l = _cast(w_in, layer, reorder_cols=(da_col0, 2 * da_width))
        p3 = _proj_in(x2, mix_norm[layer], w_in_l).reshape(b, s, -1)

        y_a = _hgrn(p3, lb_all[:, layer], hgrn_norm[layer], hg_width)

        lam_init = 0.8 - 0.6 * math.exp(-0.3 * layer)
        lam = (jnp.exp(jnp.sum(da_lambda_q1[layer].astype(F32) * da_lambda_k1[layer].astype(F32)))
               - jnp.exp(jnp.sum(da_lambda_q2[layer].astype(F32) * da_lambda_k2[layer].astype(F32)))
               + lam_init)
        y_b = _attn(p3, cos_t, sin_t, lam, da_q_norm[layer], da_k_norm[layer], da_out_norm[layer],
                    da_col0, da_width, 1.0 - lam_init)

        y_c = _conv(p3, conv_w[layer], conv_b[layer], conv_norm[layer], sc_col0, sc_width)

        x2 = _proj_out(x2, y_a.reshape(b * s, -1), y_b.reshape(b * s, -1), y_c.reshape(b * s, -1),
                       _cast(w_out, layer))

        x2 = _ffn(x2, ffn2_norm[layer], *_ffn_weights(ffn2_w_gate, ffn2_w_up, ffn2_w_down, layer))
    return x2.reshape(b, s, d)
```
